```python
import jax, jax.numpy as jnp
from jax import lax
import numpy as np

D_MODEL = 1024
BATCH = 8
SEQ = 4096
DEPTH = 4

GRID_W = 64
CTX_LEN = 256
N_MOD = 9
RET_HEADS = 4
RET_DK = 256
RET_DV = 512
RET_CHUNK = 128
ATT_HEADS = 8
ATT_KV_HEADS = 2
ATT_HD = 128
Q_BLOCK = 128
FFN_DIM = 2816
ROPE_THETA = 10000.0
EPS = 1e-6
N_RET = (DEPTH + 1) // 2
N_ATT = DEPTH // 2
ADA_SCALE = 0.5

kernel_name = "hybrid_retention_gqa_macaron_dit"

F32 = jnp.float32


def _rms(x, g):
    xf = x.astype(F32)
    y = xf * lax.rsqrt(jnp.mean(xf * xf, axis=-1, keepdims=True) + EPS)
    return (y * g.astype(F32)).astype(x.dtype)


def _modulated(x, g, shift, scale):
    return _rms(x, g) * (1.0 + scale) + shift


def _swiglu(h, w1, w2):
    gate, up = jnp.split(h @ w1, 2, axis=-1)
    return (jax.nn.silu(gate) * up) @ w2


def _rotate(x, ang):
    half = x.shape[-1] // 2
    xf = x.astype(F32)
    x1, x2 = xf[..., :half], xf[..., half:]
    cos, sin = jnp.cos(ang), jnp.sin(ang)
    return jnp.concatenate([x1 * cos - x2 * sin, x2 * cos + x1 * sin], axis=-1).astype(x.dtype)


def _axial_angles(n_tok, dim):
    rows = n_tok // GRID_W
    r, cl = jnp.meshgrid(jnp.arange(rows), jnp.arange(GRID_W), indexing="ij")
    r = r.reshape(-1).astype(F32)
    cl = cl.reshape(-1).astype(F32)
    quarter = dim // 4
    freqs = ROPE_THETA ** (-jnp.arange(quarter, dtype=F32) / quarter)
    return jnp.concatenate([r[:, None] * freqs, cl[:, None] * freqs], axis=-1)


def _seq_angles(n_tok, dim):
    half = dim // 2
    freqs = ROPE_THETA ** (-jnp.arange(half, dtype=F32) / half)
    return jnp.arange(n_tok, dtype=F32)[:, None] * freqs


def _ret_decays(log_g, C):
    pos = jnp.arange(C, dtype=F32)
    diff = pos[:, None] - pos[None, :]
    intra = jnp.where(diff[None] >= 0, jnp.exp(jnp.maximum(diff, 0.0)[None] * log_g[:, None, None]), 0.0)
    xi = jnp.exp((pos + 1.0)[None] * log_g[:, None])
    zeta = jnp.exp((C - 1.0 - pos)[None] * log_g[:, None])
    chunk = jnp.exp(C * log_g)
    return intra, xi, zeta, chunk


def _ret_scan(q, k, v, log_g, state0):
    B, H, L, dk = q.shape
    dv = v.shape[-1]
    C = RET_CHUNK
    N = L // C
    intra, xi, zeta, chunk = _ret_decays(log_g, C)
    qc = q.reshape(B, H, N, C, dk)
    kc = k.reshape(B, H, N, C, dk)
    vc = v.reshape(B, H, N, C, dv)
    s = jnp.einsum("bhncd,bhnmd->bhncm", qc, kc) * intra[None, :, None]
    inner = jnp.einsum("bhncm,bhnme->bhnce", s, vc)
    xs = (jnp.moveaxis(qc * xi[None, :, None, :, None], 2, 0),
          jnp.moveaxis(kc * zeta[None, :, None, :, None], 2, 0),
          jnp.moveaxis(vc, 2, 0))

    def step(state, blk):
        qx, kz, vb = blk
        cross = jnp.einsum("bhcd,bhde->bhce", qx, state)
        state = state * chunk[None, :, None, None] + jnp.einsum("bhcd,bhce->bhde", kz, vb)
        return state, cross

    final, cross = lax.scan(step, state0, xs)
    out = inner + jnp.moveaxis(cross, 0, 2)
    return out.reshape(B, H, L, dv), final


def _ret_final_state(k, v, log_g):
    L = k.shape[2]
    w = jnp.exp((L - 1.0 - jnp.arange(L, dtype=F32))[None] * log_g[:, None])
    return jnp.einsum("bhld,bhle,hl->bhde", k, v, w)


def _to_heads(p, n, d):
    B, L, _ = p.shape
    return p.reshape(B, L, n, d).transpose(0, 2, 1, 3)


def _retention(h, hc, w_in, w_out, dec_f, dec_b, ctx_out):
    QD = RET_HEADS * RET_DK
    VD = RET_HEADS * RET_DV
    log_f = -jnp.exp(dec_f.astype(F32))
    log_b = -jnp.exp(dec_b.astype(F32))
    flip = lambda a: a[:, :, ::-1]

    def qkvg(z):
        p = (z @ w_in).astype(F32)
        q, k, v, g = jnp.split(p, [QD, 2 * QD, 2 * QD + VD], axis=-1)
        return (_to_heads(q, RET_HEADS, RET_DK), _to_heads(k, RET_HEADS, RET_DK) * RET_DK ** -0.5,
                _to_heads(v, RET_HEADS, RET_DV), g)

    def merge(o, g):
        o = o * lax.rsqrt(jnp.mean(o * o, axis=-1, keepdims=True) + EPS)
        B, H, L, dv = o.shape
        o = o.transpose(0, 2, 1, 3).reshape(B, L, H * dv)
        return ((o * jax.nn.silu(g)) @ w_out).astype(h.dtype)

    B, L, _ = h.shape
    q, k, v, g = qkvg(h)
    ang = _seq_angles(L, RET_DK)
    q = _rotate(q, ang)
    k = _rotate(k, ang)

    if ctx_out:
        qc, kc, vc, gc = qkvg(hc)
        zeros = jnp.zeros((B, RET_HEADS, RET_DK, RET_DV), F32)
        oc_f, s_f = _ret_scan(qc, kc, vc, log_f, zeros)
        oc_b, s_b = _ret_scan(flip(qc), flip(kc), flip(vc), log_b, zeros)
        yc = merge(oc_f + flip(oc_b), gc)
    else:
        p = (hc @ w_in[:, QD:2 * QD + VD]).astype(F32)
        kc = _to_heads(p[..., :QD], RET_HEADS, RET_DK) * RET_DK ** -0.5
        vc = _to_heads(p[..., QD:], RET_HEADS, RET_DV)
        s_f = _ret_final_state(kc, vc, log_f)
        s_b = _ret_final_state(flip(kc), flip(vc), log_b)
        yc = None

    o_f, _ = _ret_scan(q, k, v, log_f, s_f)
    o_b, _ = _ret_scan(flip(q), flip(k), flip(v), log_b, s_b)
    y = merge(o_f + flip(o_b), g)
    return y, yc


def _attend(q, k, v):
    s = jnp.einsum("bkgqd,bksd->bkgqs", q, k).astype(F32) * ATT_HD ** -0.5
    p = jax.nn.softmax(s, axis=-1)
    return jnp.einsum("bkgqs,bksd->bkgqd", p.astype(v.dtype), v)


def _gqa(h, hc, w_qkv, w_o, q_gain, k_gain, ctx_out):
    QD = ATT_HEADS * ATT_HD
    KD = ATT_KV_HEADS * ATT_HD
    G = ATT_HEADS // ATT_KV_HEADS

    def kv(p):
        k = _rms(_to_heads(p[..., :KD], ATT_KV_HEADS, ATT_HD), k_gain)
        v = _to_heads(p[..., KD:], ATT_KV_HEADS, ATT_HD)
        return k, v

    def merge(o):
        B, _, _, L, _ = o.shape
        return o.reshape(B, ATT_HEADS, L, ATT_HD).transpose(0, 2, 1, 3).reshape(B, L, QD) @ w_o

    B, L, _ = h.shape
    p = h @ w_qkv
    q = _rms(_to_heads(p[..., :QD], ATT_HEADS, ATT_HD), q_gain)
    k, v = kv(p[..., QD:])
    ang = _axial_angles(L, ATT_HD)
    q = _rotate(q, ang)
    k = _rotate(k, ang)

    if ctx_out:
        pc = hc @ w_qkv
        qc = _rms(_to_heads(pc[..., :QD], ATT_HEADS, ATT_HD), q_gain)
        kc, vc = kv(pc[..., QD:])
        Lc = hc.shape[1]
        yc = merge(_attend(qc.reshape(B, ATT_KV_HEADS, G, Lc, ATT_HD), kc, vc))
    else:
        kc, vc = kv(hc @ w_qkv[:, QD:])
        yc = None

    k_all = jnp.concatenate([k, kc], axis=2)
    v_all = jnp.concatenate([v, vc], axis=2)
    NB = L // Q_BLOCK
    qb = jnp.moveaxis(q.reshape(B, ATT_KV_HEADS, G, NB, Q_BLOCK, ATT_HD), 3, 0)
    o = lax.map(lambda qi: _attend(qi, k_all, v_all), qb)
    o = jnp.moveaxis(o, 0, 3).reshape(B, ATT_KV_HEADS, G, L, ATT_HD)
    return merge(o), yc


def setup_inputs(seed: int = 0) -> dict:
    key = jax.random.key(seed)
    ks = jax.random.split(key, 20)
    D = D_MODEL
    nrm = lambda k, shp, fan: jax.random.normal(k, shp, F32) * fan ** -0.5
    ret_in = RET_HEADS * (2 * RET_DK + 2 * RET_DV)
    att_in = (ATT_HEADS + 2 * ATT_KV_HEADS) * ATT_HD
    base = jnp.log(-jnp.log1p(-(2.0 ** (-5.0 - jnp.arange(RET_HEADS, dtype=F32)))))
    return {
        "x": jax.random.normal(ks[0], (BATCH, SEQ, D), F32),
        "c": jax.random.normal(ks[1], (BATCH, D), F32),
        "ctx": jax.random.normal(ks[2], (BATCH, CTX_LEN, D), F32),
        "c_ctx": jax.random.normal(ks[3], (D,), F32),
        "ada_w": nrm(ks[4], (DEPTH, D, N_MOD * D), D) * ADA_SCALE,
        "ada_b": 0.01 * jax.random.normal(ks[5], (DEPTH, N_MOD * D), F32),
        "norm_g": 1.0 + 0.02 * jax.random.normal(ks[6], (DEPTH, 3, D), F32),
        "ffn_w1": nrm(ks[7], (DEPTH, 2, D, 2 * FFN_DIM), D),
        "ffn_w2": nrm(ks[8], (DEPTH, 2, FFN_DIM, D), FFN_DIM),
        "ret_w_in": nrm(ks[9], (N_RET, D, ret_in), D),
        "ret_w_out": nrm(ks[10], (N_RET, RET_HEADS * RET_DV, D), RET_HEADS * RET_DV),
        "ret_decay_f": base[None] + 0.01 * jax.random.normal(ks[11], (N_RET, RET_HEADS), F32),
        "ret_decay_b": base[None] + 0.01 * jax.random.normal(ks[12], (N_RET, RET_HEADS), F32),
        "att_w_qkv": nrm(ks[13], (N_ATT, D, att_in), D),
        "att_w_o": nrm(ks[14], (N_ATT, ATT_HEADS * ATT_HD, D), ATT_HEADS * ATT_HD),
        "att_q_gain": 1.0 + 0.02 * jax.random.normal(ks[15], (N_ATT, ATT_HD), F32),
        "att_k_gain": 1.0 + 0.02 * jax.random.normal(ks[16], (N_ATT, ATT_HD), F32),
        "final_g": 1.0 + 0.02 * jax.random.normal(ks[17], (D,), F32),
    }


def reference(x, c, ctx, c_ctx, ada_w, ada_b, norm_g, ffn_w1, ffn_w2, ret_w_in, ret_w_out,
              ret_decay_f, ret_decay_b, att_w_qkv, att_w_o, att_q_gain, att_k_gain, final_g):
    xc = ctx
    sc = jax.nn.silu(c)
    scc = jax.nn.silu(c_ctx)
    for i in range(DEPTH):
        last = i == DEPTH - 1
        m = jnp.split((sc @ ada_w[i] + ada_b[i])[:, None, :], N_MOD, axis=-1)
        mc = jnp.split(scc @ ada_w[i] + ada_b[i], N_MOD, axis=-1)
        x = x + 0.5 * m[2] * _swiglu(_modulated(x, norm_g[i, 0], m[0], m[1]), ffn_w1[i, 0], ffn_w2[i, 0])
        xc = xc + 0.5 * mc[2] * _swiglu(_modulated(xc, norm_g[i, 0], mc[0], mc[1]), ffn_w1[i, 0], ffn_w2[i, 0])
        h = _modulated(x, norm_g[i, 1], m[3], m[4])
        hc = _modulated(xc, norm_g[i, 1], mc[3], mc[4])
        j = i // 2
        if i % 2 == 0:
            y, yc = _retention(h, hc, ret_w_in[j], ret_w_out[j], ret_decay_f[j], ret_decay_b[j], not last)
        else:
            y, yc = _gqa(h, hc, att_w_qkv[j], att_w_o[j], att_q_gain[j], att_k_gain[j], not last)
        x = x + m[5] * y
        if not last:
            xc = xc + mc[5] * yc
            xc = xc + 0.5 * mc[8] * _swiglu(_modulated(xc, norm_g[i, 2], mc[6], mc[7]), ffn_w1[i, 1], ffn_w2[i, 1])
        x = x + 0.5 * m[8] * _swiglu(_modulated(x, norm_g[i, 2], m[6], m[7]), ffn_w1[i, 1], ffn_w2[i, 1])
    return _rms(x, final_g)
```

```python
import functools

import jax
import jax.numpy as jnp
from jax import lax
from jax.experimental import pallas as pl
from jax.experimental.pallas import tpu as pltpu

F32 = jnp.float32
BF16 = jnp.bfloat16

N_MOD = 9
RET_HEADS = 4
RET_DK = 256
RET_DV = 512
ATT_HEADS = 8
ATT_KV_HEADS = 2
ATT_HD = 128
GRID_W = 64
ROPE_THETA = 10000.0
EPS = 1e-6

ROW_TILE = 512
RET_CHUNK = 512
ATT_Q_TILE = 256
ADA_ROWS = 16
VMEM_LIMIT = 56 * 1024 * 1024


def _params(n_axes):
    return pltpu.CompilerParams(
        dimension_semantics=("arbitrary",) * n_axes, vmem_limit_bytes=VMEM_LIMIT)


def _resident(shape):
    nd = len(shape)
    return pl.BlockSpec(shape, lambda *_: (0,) * nd, pipeline_mode=pl.Buffered(1))


def _silu(x):
    return x * jax.nn.sigmoid(x)


def _rms_rows(x):
    return x * lax.rsqrt(jnp.mean(x * x, axis=-1, keepdims=True) + EPS)


def _modulated(x, gain, shift, scale):
    return (_rms_rows(x) * gain) * (1.0 + scale) + shift


def _dot(a, b):
    return jnp.dot(a, b, preferred_element_type=F32)


def _dot_nt(a, b):
    return lax.dot_general(a, b, (((1,), (1,)), ((), ())), preferred_element_type=F32)


def _dot_tn(a, b):
    return lax.dot_general(a, b, (((0,), (0,)), ((), ())), preferred_element_type=F32)


def _ada_kernel(cc_ref, w_ref, b_ref, o_ref):
    s = _silu(cc_ref[...])
    o_ref[0] = _dot(s.astype(BF16), w_ref[0].astype(BF16)) + b_ref[0]


def _ada_call(cc, ada_w, ada_b):
    depth, d, n = ada_w.shape
    tn = 1024
    return pl.pallas_call(
        _ada_kernel,
        grid=(depth, n // tn),
        in_specs=[
            pl.BlockSpec((ADA_ROWS, d), lambda l, j: (0, 0)),
            pl.BlockSpec((1, d, tn), lambda l, j: (l, 0, j)),
            pl.BlockSpec((1, 1, tn), lambda l, j: (l, 0, j)),
        ],
        out_specs=pl.BlockSpec((1, ADA_ROWS, tn), lambda l, j: (l, 0, j)),
        out_shape=jax.ShapeDtypeStruct((depth, ADA_ROWS, n), F32),
        compiler_params=_params(2),
        name="ada",
    )(cc, ada_w, ada_b.reshape(depth, 1, n))


class _Rows:
    def __init__(self, batch, seq, ctx_len):
        self.batch, self.seq, self.ctx_len = batch, seq, ctx_len
        assert seq % ROW_TILE == 0 and (batch * ctx_len) % ROW_TILE == 0
        self.tiles_per_batch = seq // ROW_TILE
        self.n_lat = batch * seq // ROW_TILE
        self.n_ctx = batch * ctx_len // ROW_TILE
        self.n_all = self.n_lat + self.n_ctx

    def mod_spec(self, layer, d):
        n_lat, tpb, batch = self.n_lat, self.tiles_per_batch, self.batch
        return pl.BlockSpec(
            (1, 1, N_MOD, d),
            lambda i: (layer, jnp.where(i < n_lat, i // tpb, batch), 0, 0))

    def rope_spec(self, width):
        n_lat, tpb = self.n_lat, self.tiles_per_batch
        return pl.BlockSpec((ROW_TILE, width), lambda i: (jnp.where(i < n_lat, i % tpb, tpb), 0))

    @staticmethod
    def row_spec(width):
        return pl.BlockSpec((ROW_TILE, width), lambda i: (i, 0))


def _mod_rows(mod_ref, base):
    return [mod_ref[0, 0, base + k:base + k + 1, :] for k in range(3)]


def _ffn_kernel(x_ref, mod_ref, g_ref, w1_ref, w2_ref, *rest, base, ffn_dim, chunk, final):
    if final:
        fg_ref, o_ref = rest
    else:
        (o_ref,) = rest
    x = x_ref[...]
    shift, scale, gate = _mod_rows(mod_ref, base)
    h = _modulated(x, g_ref[...], shift, scale).astype(BF16)
    acc = None
    for c in range(ffn_dim // chunk):
        lo = c * chunk
        gt = _dot(h, w1_ref[:, lo:lo + chunk])
        up = _dot(h, w1_ref[:, ffn_dim + lo:ffn_dim + lo + chunk])
        a = (_silu(gt) * up).astype(BF16)
        part = _dot(a, w2_ref[lo:lo + chunk, :])
        acc = part if acc is None else acc + part
    y = x + (0.5 * gate) * acc
    if final:
        y = _rms_rows(y) * fg_ref[...]
    o_ref[...] = y


def _ffn_call(rows, n_tiles, x, mod, layer, base, gain, w1, w2, final_g=None):
    d = x.shape[1]
    ffn_dim = w2.shape[0]
    chunk = ffn_dim // 2
    assert chunk % 128 == 0
    final = final_g is not None
    in_specs = [rows.row_spec(d), rows.mod_spec(layer, d), pl.BlockSpec((1, d), lambda i: (0, 0)),
                _resident(w1.shape), _resident(w2.shape)]
    args = [x, mod, gain.reshape(1, d), w1, w2]
    if final:
        in_specs.append(pl.BlockSpec((1, d), lambda i: (0, 0)))
        args.append(final_g.reshape(1, d))
    return pl.pallas_call(
        functools.partial(_ffn_kernel, base=base, ffn_dim=ffn_dim, chunk=chunk, final=final),
        grid=(n_tiles,),
        in_specs=in_specs,
        out_specs=rows.row_spec(d),
        out_shape=jax.ShapeDtypeStruct((n_tiles * ROW_TILE, d), F32),
        compiler_params=_params(1),
        name="ffn",
    )(*args)


def _rope_halves(x1, x2, cos, sin):
    return x1 * cos - x2 * sin, x2 * cos + x1 * sin


def _proj_ret_kernel(x_ref, mod_ref, g_ref, w_ref, cos_ref, sin_ref, o_ref):
    shift, scale, _ = _mod_rows(mod_ref, 3)
    h = _modulated(x_ref[...], g_ref[...], shift, scale).astype(BF16)
    cos, sin = cos_ref[...], sin_ref[...]
    qd = RET_HEADS * RET_DK
    vd = RET_HEADS * RET_DV
    half = RET_DK // 2
    for part, mult in ((0, None), (1, RET_DK ** -0.5)):
        p = _dot(h, w_ref[:, part * qd:(part + 1) * qd])
        if mult is not None:
            p = p * mult
        for hh in range(RET_HEADS):
            lo = hh * RET_DK
            r1, r2 = _rope_halves(p[:, lo:lo + half], p[:, lo + half:lo + RET_DK], cos, sin)
            o_ref[:, part * qd + lo:part * qd + lo + half] = r1.astype(BF16)
            o_ref[:, part * qd + lo + half:part * qd + lo + RET_DK] = r2.astype(BF16)
    for part in range(2):
        lo = 2 * qd + part * vd
        o_ref[:, lo:lo + vd] = _dot(h, w_ref[:, lo:lo + vd]).astype(BF16)


def _proj_att_kernel(x_ref, mod_ref, g_ref, w_ref, qg_ref, kg_ref, cos_ref, sin_ref, o_ref):
    shift, scale, _ = _mod_rows(mod_ref, 3)
    h = _modulated(x_ref[...], g_ref[...], shift, scale).astype(BF16)
    cos, sin = cos_ref[...], sin_ref[...]
    qd = ATT_HEADS * ATT_HD
    kd = ATT_KV_HEADS * ATT_HD

    def norm_rope(p, n_heads, gain, out_lo):
        for hh in range(n_heads):
            lo = hh * ATT_HD
            y = _rms_rows(p[:, lo:lo + ATT_HD]) * gain
            r = y * cos + pltpu.roll(y, ATT_HD // 2, 1) * sin
            o_ref[:, out_lo + lo:out_lo + lo + ATT_HD] = r.astype(BF16)

    norm_rope(_dot(h, w_ref[:, :qd]), ATT_HEADS, qg_ref[...], 0)
    norm_rope(_dot(h, w_ref[:, qd:qd + kd]), ATT_KV_HEADS, kg_ref[...], qd)
    o_ref[:, qd + kd:] = _dot(h, w_ref[:, qd + kd:]).astype(BF16)


def _proj_call(kernel, name, rows, x, mod, layer, gain, w, extras, extra_specs):
    d = x.shape[1]
    n_out = w.shape[1]
    return pl.pallas_call(
        kernel,
        grid=(rows.n_all,),
        in_specs=[rows.row_spec(d), rows.mod_spec(layer, d), pl.BlockSpec((1, d), lambda i: (0, 0)),
                  _resident(w.shape)] + extra_specs,
        out_specs=rows.row_spec(n_out),
        out_shape=jax.ShapeDtypeStruct((rows.n_all * ROW_TILE, n_out), BF16),
        compiler_params=_params(1),
        name=name,
    )(x, mod, gain.reshape(1, d), w, *extras)


def _decay_tables(n, lg_f, lg_b):
    i = lax.broadcasted_iota(jnp.int32, (n, n), 0)
    j = lax.broadcasted_iota(jnp.int32, (n, n), 1)
    diff = (i - j).astype(F32)
    mask = (jnp.where(diff >= 0, jnp.exp(jnp.maximum(diff, 0.0) * lg_f), 0.0)
            + jnp.where(diff <= 0, jnp.exp(jnp.maximum(-diff, 0.0) * lg_b), 0.0))
    pos = lax.broadcasted_iota(jnp.int32, (n, 1), 0).astype(F32)
    xi_f = jnp.exp((pos + 1.0) * lg_f)
    xi_b = jnp.exp((n - pos) * lg_b)
    zeta_f = jnp.exp((n - 1.0 - pos) * lg_f)
    zeta_b = jnp.exp(pos * lg_b)
    return mask, xi_f, xi_b, zeta_f, zeta_b


def _scaled_bf16(x, col):
    return (x.astype(F32) * col).astype(BF16)


def _head_norm_gate(o, g):
    return (_rms_rows(o) * _silu(g.astype(F32))).astype(BF16)


def _ret_kernel(ql_ref, kl_ref, vl_ref, gl_ref, qc_ref, kc_ref, vc_ref, gc_ref, dec_ref,
                ol_ref, oc_ref, sf_ref, sb_ref, run_ref, *, n_chunks, chunk, ctx_len):
    lg_f = -jnp.exp(dec_ref[0, 0:1, 0:1])
    lg_b = -jnp.exp(dec_ref[0, 1:2, 0:1])

    mask, _, _, zeta_f, zeta_b = _decay_tables(ctx_len, lg_f, lg_b)
    q, k, v = qc_ref[...], kc_ref[...], vc_ref[...]
    p = (_dot_nt(q, k) * mask).astype(BF16)
    oc_ref[...] = _head_norm_gate(_dot(p, v), gc_ref[...])
    s_f = _dot_tn(_scaled_bf16(k, zeta_f), v)
    s_b = _dot_tn(_scaled_bf16(k, zeta_b), v)

    mask, xi_f, xi_b, zeta_f, zeta_b = _decay_tables(chunk, lg_f, lg_b)
    decay_f = jnp.exp(chunk * lg_f)
    decay_b = jnp.exp(chunk * lg_b)

    def rows(n):
        return pl.ds(pl.multiple_of(n * chunk, chunk), chunk)

    run_ref[...] = s_f
    sf_ref[0] = s_f.astype(BF16)

    def fwd(n, carry):
        kz = _scaled_bf16(kl_ref[rows(n), :], zeta_f)
        s = run_ref[...] * decay_f + _dot_tn(kz, vl_ref[rows(n), :])
        run_ref[...] = s
        sf_ref[n + 1] = s.astype(BF16)
        return carry

    lax.fori_loop(0, n_chunks - 1, fwd, 0)

    run_ref[...] = s_b
    sb_ref[n_chunks - 1] = s_b.astype(BF16)

    def bwd(t, carry):
        n = n_chunks - 1 - t
        kz = _scaled_bf16(kl_ref[rows(n), :], zeta_b)
        s = run_ref[...] * decay_b + _dot_tn(kz, vl_ref[rows(n), :])
        run_ref[...] = s
        sb_ref[n - 1] = s.astype(BF16)
        return carry

    lax.fori_loop(0, n_chunks - 1, bwd, 0)

    def out(n, carry):
        r = rows(n)
        q, k, v = ql_ref[r, :], kl_ref[r, :], vl_ref[r, :]
        p = (_dot_nt(q, k) * mask).astype(BF16)
        o = _dot(p, v)
        o = o + _dot(_scaled_bf16(q, xi_f), sf_ref[n])
        o = o + _dot(_scaled_bf16(q, xi_b), sb_ref[n])
        ol_ref[r, :] = _head_norm_gate(o, gl_ref[r, :])
        return carry

    lax.fori_loop(0, n_chunks, out, 0)


def _ret_call(batch, seq, ctx_len, proj, dec_tab):
    chunk = min(RET_CHUNK, seq)
    assert seq % chunk == 0
    n_chunks = seq // chunk
    ctx_row0 = batch * seq // ctx_len
    kq = RET_HEADS
    kv = 2 * RET_HEADS * RET_DK // RET_DV
    kg = kv + RET_HEADS
    vd = RET_HEADS * RET_DV
    in_specs = [
        pl.BlockSpec((seq, RET_DK), lambda b, h: (b, h)),
        pl.BlockSpec((seq, RET_DK), lambda b, h: (b, kq + h)),
        pl.BlockSpec((seq, RET_DV), lambda b, h: (b, kv + h)),
        pl.BlockSpec((seq, RET_DV), lambda b, h: (b, kg + h)),
        pl.BlockSpec((ctx_len, RET_DK), lambda b, h: (ctx_row0 + b, h)),
        pl.BlockSpec((ctx_len, RET_DK), lambda b, h: (ctx_row0 + b, kq + h)),
        pl.BlockSpec((ctx_len, RET_DV), lambda b, h: (ctx_row0 + b, kv + h)),
        pl.BlockSpec((ctx_len, RET_DV), lambda b, h: (ctx_row0 + b, kg + h)),
        pl.BlockSpec((1, 8, 128), lambda b, h: (h, 0, 0)),
    ]
    return pl.pallas_call(
        functools.partial(_ret_kernel, n_chunks=n_chunks, chunk=chunk, ctx_len=ctx_len),
        grid=(batch, RET_HEADS),
        in_specs=in_specs,
        out_specs=[pl.BlockSpec((seq, RET_DV), lambda b, h: (b, h)),
                   pl.BlockSpec((ctx_len, RET_DV), lambda b, h: (b, h))],
        out_shape=[jax.ShapeDtypeStruct((batch * seq, vd), BF16),
                   jax.ShapeDtypeStruct((batch * ctx_len, vd), BF16)],
        scratch_shapes=[pltpu.VMEM((n_chunks, RET_DK, RET_DV), BF16),
                        pltpu.VMEM((n_chunks, RET_DK, RET_DV), BF16),
                        pltpu.VMEM((RET_DK, RET_DV), F32)],
        compiler_params=_params(2),
        name="ret",
    )(proj, proj, proj, proj, proj, proj, proj, proj, dec_tab)


def _att_kernel(q_ref, kl_ref, vl_ref, kc_ref, vc_ref, o_ref, *, n_q):
    qi = pl.program_id(2)
    group = ATT_HEADS // ATT_KV_HEADS
    scale = ATT_HD ** -0.5

    def attend(keys, values):
        for g in range(group):
            q = q_ref[:, g * ATT_HD:(g + 1) * ATT_HD]
            scores = [_dot_nt(q, k_ref[...]) * scale for k_ref in keys]
            m = functools.reduce(jnp.maximum, [jnp.max(s, axis=-1, keepdims=True) for s in scores])
            es = [jnp.exp(s - m) for s in scores]
            denom = functools.reduce(jnp.add, [jnp.sum(e, axis=-1, keepdims=True) for e in es])
            o = functools.reduce(
                jnp.add, [_dot(e.astype(BF16), v_ref[...]) for e, v_ref in zip(es, values)])
            o_ref[:, g * ATT_HD:(g + 1) * ATT_HD] = (o / denom).astype(BF16)

    @pl.when(qi < n_q)
    def _():
        attend([kl_ref, kc_ref], [vl_ref, vc_ref])

    @pl.when(qi >= n_q)
    def _():
        attend([kc_ref], [vc_ref])


def _att_call(batch, seq, ctx_len, proj, ctx_out):
    tq = ATT_Q_TILE
    assert ctx_len == tq and seq % tq == 0
    n_q = seq // tq
    ctx_row0 = batch * seq // ctx_len
    group_w = (ATT_HEADS // ATT_KV_HEADS) * ATT_HD
    k_col = ATT_HEADS
    v_col = ATT_HEADS + ATT_KV_HEADS
    n_rows = batch * seq + (batch * ctx_len if ctx_out else 0)

    def q_map(b, kh, qi):
        return (jnp.where(qi < n_q, b * n_q + qi, ctx_row0 + b), kh)

    return pl.pallas_call(
        functools.partial(_att_kernel, n_q=n_q),
        grid=(batch, ATT_KV_HEADS, n_q + (1 if ctx_out else 0)),
        in_specs=[
            pl.BlockSpec((tq, group_w), q_map),
            pl.BlockSpec((seq, ATT_HD), lambda b, kh, qi: (b, k_col + kh)),
            pl.BlockSpec((seq, ATT_HD), lambda b, kh, qi: (b, v_col + kh)),
            pl.BlockSpec((ctx_len, ATT_HD), lambda b, kh, qi: (ctx_row0 + b, k_col + kh)),
            pl.BlockSpec((ctx_len, ATT_HD), lambda b, kh, qi: (ctx_row0 + b, v_col + kh)),
        ],
        out_specs=pl.BlockSpec((tq, group_w), q_map),
        out_shape=jax.ShapeDtypeStruct((n_rows, ATT_HEADS * ATT_HD), BF16),
        compiler_params=_params(3),
        name="att",
    )(proj, proj, proj, proj, proj)


def _oproj_kernel(x_ref, mod_ref, w_ref, *rest, n_lat):
    gate = mod_ref[0, 0, 5:6, :]
    if len(rest) == 2:
        a_ref, o_ref = rest
        o_ref[...] = x_ref[...] + gate * _dot(a_ref[...], w_ref[...])
        return
    al_ref, ac_ref, o_ref = rest
    i = pl.program_id(0)

    @pl.when(i < n_lat)
    def _():
        o_ref[...] = x_ref[...] + gate * _dot(al_ref[...], w_ref[...])

    @pl.when(i >= n_lat)
    def _():
        o_ref[...] = x_ref[...] + gate * _dot(ac_ref[...], w_ref[...])


def _oproj_call(rows, n_tiles, x, mod, layer, w, a_lat, a_ctx=None):
    d = x.shape[1]
    k = w.shape[0]
    n_lat = rows.n_lat
    in_specs = [rows.row_spec(d), rows.mod_spec(layer, d), _resident(w.shape)]
    args = [x, mod, w]
    if a_ctx is None:
        in_specs.append(rows.row_spec(k))
        args.append(a_lat)
    else:
        in_specs.append(pl.BlockSpec((ROW_TILE, k), lambda i: (jnp.minimum(i, n_lat - 1), 0)))
        in_specs.append(pl.BlockSpec((ROW_TILE, k), lambda i: (jnp.maximum(i - n_lat, 0), 0)))
        args += [a_lat, a_ctx]
    return pl.pallas_call(
        functools.partial(_oproj_kernel, n_lat=n_lat),
        grid=(n_tiles,),
        in_specs=in_specs,
        out_specs=rows.row_spec(d),
        out_shape=jax.ShapeDtypeStruct((n_tiles * ROW_TILE, d), F32),
        compiler_params=_params(1),
        name="oproj",
    )(*args)


def _with_identity_rows(cos, sin):
    pad = (ROW_TILE, cos.shape[1])
    return (jnp.concatenate([cos, jnp.ones(pad, F32)], axis=0),
            jnp.concatenate([sin, jnp.zeros(pad, F32)], axis=0))


def _seq_rope_tables(seq):
    half = RET_DK // 2
    freqs = ROPE_THETA ** (-jnp.arange(half, dtype=F32) / half)
    ang = jnp.arange(seq, dtype=F32)[:, None] * freqs
    return _with_identity_rows(jnp.cos(ang), jnp.sin(ang))


def _axial_rope_tables(seq):
    quarter = ATT_HD // 4
    tok = jnp.arange(seq)
    r = (tok // GRID_W).astype(F32)
    cl = (tok % GRID_W).astype(F32)
    freqs = ROPE_THETA ** (-jnp.arange(quarter, dtype=F32) / quarter)
    ang = jnp.concatenate([r[:, None] * freqs, cl[:, None] * freqs], axis=-1)
    cos, sin = jnp.cos(ang), jnp.sin(ang)
    return _with_identity_rows(jnp.concatenate([cos, cos], axis=-1),
                               jnp.concatenate([-sin, sin], axis=-1))


def kernel(x, c, ctx, c_ctx, ada_w, ada_b, norm_g, ffn_w1, ffn_w2, ret_w_in, ret_w_out,
           ret_decay_f, ret_decay_b, att_w_qkv, att_w_o, att_q_gain, att_k_gain, final_g):
    batch, seq, d = x.shape
    ctx_len = ctx.shape[1]
    depth = ada_w.shape[0]
    rows = _Rows(batch, seq, ctx_len)

    cc = jnp.concatenate(
        [c, c_ctx[None], jnp.zeros((ADA_ROWS - batch - 1, d), F32)], axis=0)
    mod = _ada_call(cc, ada_w, ada_b).reshape(depth, ADA_ROWS, N_MOD, d)

    w1 = ffn_w1.astype(BF16)
    w2 = ffn_w2.astype(BF16)
    w_in = ret_w_in.astype(BF16)
    w_out = ret_w_out.astype(BF16)
    w_qkv = att_w_qkv.astype(BF16)
    w_o = att_w_o.astype(BF16)

    ret_cos, ret_sin = _seq_rope_tables(seq)
    att_cos, att_sin = _axial_rope_tables(seq)
    ret_rope_specs = [rows.rope_spec(RET_DK // 2)] * 2
    att_rope_specs = [rows.rope_spec(ATT_HD)] * 2
    gain_spec = pl.BlockSpec((1, ATT_HD), lambda i: (0, 0))

    xs = jnp.concatenate([x.reshape(batch * seq, d), ctx.reshape(batch * ctx_len, d)], axis=0)

    for i in range(depth):
        last = i == depth - 1
        j = i // 2
        xs = _ffn_call(rows, rows.n_all, xs, mod, i, 0, norm_g[i, 0], w1[i, 0], w2[i, 0])
        n_tiles = rows.n_lat if last else rows.n_all
        if i % 2 == 0:
            proj = _proj_call(_proj_ret_kernel, "proj_ret", rows, xs, mod, i, norm_g[i, 1],
                              w_in[j], [ret_cos, ret_sin], ret_rope_specs)
            dec_tab = jnp.broadcast_to(
                jnp.stack([ret_decay_f[j], ret_decay_b[j]], axis=1)[:, :, None],
                (RET_HEADS, 2, 128))
            dec_tab = jnp.concatenate([dec_tab, jnp.zeros((RET_HEADS, 6, 128), F32)], axis=1)
            a_lat, a_ctx = _ret_call(batch, seq, ctx_len, proj, dec_tab)
            xs = _oproj_call(rows, n_tiles, xs, mod, i, w_out[j], a_lat,
                             None if last else a_ctx)
        else:
            proj = _proj_call(_proj_att_kernel, "proj_att", rows, xs, mod, i, norm_g[i, 1],
                              w_qkv[j],
                              [att_q_gain[j].reshape(1, ATT_HD), att_k_gain[j].reshape(1, ATT_HD),
                               att_cos, att_sin],
                              [gain_spec, gain_spec] + att_rope_specs)
            a = _att_call(batch, seq, ctx_len, proj, not last)
            xs = _oproj_call(rows, n_tiles, xs, mod, i, w_o[j], a)
        xs = _ffn_call(rows, n_tiles, xs, mod, i, 6, norm_g[i, 2], w1[i, 1], w2[i, 1],
                       final_g if last else None)
    return xs.reshape(batch, seq, d)
```

```python
import functools

import jax
import jax.numpy as jnp
from jax import lax
from jax.experimental import pallas as pl
from jax.experimental.pallas import tpu as pltpu

F32 = jnp.float32
BF16 = jnp.bfloat16

N_MOD = 9
RET_HEADS = 4
RET_DK = 256
RET_DV = 512
ATT_HEADS = 8
ATT_KV_HEADS = 2
ATT_HD = 128
GRID_W = 64
ROPE_THETA = 10000.0
EPS = 1e-6

ROW_TILE = 512
RET_CHUNK = 512
ATT_Q_TILE = 256
ATT_ROW_GROUP = 256
LANES = 128
ADA_ROWS = 16
VMEM_LIMIT = 56 * 1024 * 1024


def _params(n_axes):
    return pltpu.CompilerParams(
        dimension_semantics=("arbitrary",) * n_axes, vmem_limit_bytes=VMEM_LIMIT)


def _resident(shape):
    nd = len(shape)
    return pl.BlockSpec(shape, lambda *_: (0,) * nd, pipeline_mode=pl.Buffered(1))


def _silu(x):
    return x * jax.nn.sigmoid(x)


def _rms_rows(x):
    return x * lax.rsqrt(jnp.mean(x * x, axis=-1, keepdims=True) + EPS)


def _modulated(x, gain, shift, scale):
    return (_rms_rows(x) * gain) * (1.0 + scale) + shift


def _dot(a, b):
    return jnp.dot(a, b, preferred_element_type=F32)


def _dot_nt(a, b):
    return lax.dot_general(a, b, (((1,), (1,)), ((), ())), preferred_element_type=F32)


def _dot_tn(a, b):
    return lax.dot_general(a, b, (((0,), (0,)), ((), ())), preferred_element_type=F32)


def _ada_kernel(cc_ref, w_ref, b_ref, o_ref):
    s = _silu(cc_ref[...])
    o_ref[0] = _dot(s.astype(BF16), w_ref[0].astype(BF16)) + b_ref[0]


def _ada_call(cc, ada_w, ada_b):
    depth, d, n = ada_w.shape
    tn = 1024
    return pl.pallas_call(
        _ada_kernel,
        grid=(depth, n // tn),
        in_specs=[
            pl.BlockSpec((ADA_ROWS, d), lambda l, j: (0, 0)),
            pl.BlockSpec((1, d, tn), lambda l, j: (l, 0, j)),
            pl.BlockSpec((1, 1, tn), lambda l, j: (l, 0, j)),
        ],
        out_specs=pl.BlockSpec((1, ADA_ROWS, tn), lambda l, j: (l, 0, j)),
        out_shape=jax.ShapeDtypeStruct((depth, ADA_ROWS, n), F32),
        compiler_params=_params(2),
        name="ada",
    )(cc, ada_w, ada_b.reshape(depth, 1, n))


class _Rows:
    def __init__(self, batch, seq, ctx_len):
        self.batch, self.seq, self.ctx_len = batch, seq, ctx_len
        assert seq % ROW_TILE == 0 and (batch * ctx_len) % ROW_TILE == 0
        self.tiles_per_batch = seq // ROW_TILE
        self.n_lat = batch * seq // ROW_TILE
        self.n_ctx = batch * ctx_len // ROW_TILE
        self.n_all = self.n_lat + self.n_ctx

    def mod_spec(self, layer, d):
        n_lat, tpb, batch = self.n_lat, self.tiles_per_batch, self.batch
        return pl.BlockSpec(
            (1, 1, N_MOD, d),
            lambda i: (layer, jnp.where(i < n_lat, i // tpb, batch), 0, 0))

    def rope_spec(self, width):
        n_lat, tpb = self.n_lat, self.tiles_per_batch
        return pl.BlockSpec((ROW_TILE, width), lambda i: (jnp.where(i < n_lat, i % tpb, tpb), 0))

    @staticmethod
    def row_spec(width):
        return pl.BlockSpec((ROW_TILE, width), lambda i: (i, 0))


def _mod_rows(mod_ref, base):
    return [mod_ref[0, 0, base + k:base + k + 1, :] for k in range(3)]


def _ffn_kernel(x_ref, mod_ref, g_ref, w1_ref, w2_ref, *rest, base, ffn_dim, chunk, final):
    if final:
        fg_ref, o_ref = rest
    else:
        (o_ref,) = rest
    x = x_ref[...]
    shift, scale, gate = _mod_rows(mod_ref, base)
    h = _modulated(x, g_ref[...], shift, scale).astype(BF16)
    acc = None
    for c in range(ffn_dim // chunk):
        lo = c * chunk
        gt = _dot(h, w1_ref[:, lo:lo + chunk])
        up = _dot(h, w1_ref[:, ffn_dim + lo:ffn_dim + lo + chunk])
        a = (_silu(gt) * up).astype(BF16)
        part = _dot(a, w2_ref[lo:lo + chunk, :])
        acc = part if acc is None else acc + part
    y = x + (0.5 * gate) * acc
    if final:
        y = _rms_rows(y) * fg_ref[...]
    o_ref[...] = y


def _ffn_call(rows, n_tiles, x, mod, layer, base, gain, w1, w2, final_g=None):
    d = x.shape[1]
    ffn_dim = w2.shape[0]
    chunk = ffn_dim // 2
    assert chunk % 128 == 0
    final = final_g is not None
    in_specs = [rows.row_spec(d), rows.mod_spec(layer, d), pl.BlockSpec((1, d), lambda i: (0, 0)),
                _resident(w1.shape), _resident(w2.shape)]
    args = [x, mod, gain.reshape(1, d), w1, w2]
    if final:
        in_specs.append(pl.BlockSpec((1, d), lambda i: (0, 0)))
        args.append(final_g.reshape(1, d))
    return pl.pallas_call(
        functools.partial(_ffn_kernel, base=base, ffn_dim=ffn_dim, chunk=chunk, final=final),
        grid=(n_tiles,),
        in_specs=in_specs,
        out_specs=rows.row_spec(d),
        out_shape=jax.ShapeDtypeStruct((n_tiles * ROW_TILE, d), F32),
        compiler_params=_params(1),
        name="ffn",
    )(*args)


def _rope_halves(x1, x2, cos, sin):
    return x1 * cos - x2 * sin, x2 * cos + x1 * sin


def _proj_ret_kernel(x_ref, mod_ref, g_ref, w_ref, cos_ref, sin_ref, o_ref):
    shift, scale, _ = _mod_rows(mod_ref, 3)
    h = _modulated(x_ref[...], g_ref[...], shift, scale).astype(BF16)
    cos, sin = cos_ref[...], sin_ref[...]
    qd = RET_HEADS * RET_DK
    vd = RET_HEADS * RET_DV
    half = RET_DK // 2
    for part, mult in ((0, None), (1, RET_DK ** -0.5)):
        p = _dot(h, w_ref[:, part * qd:(part + 1) * qd])
        if mult is not None:
            p = p * mult
        for hh in range(RET_HEADS):
            lo = hh * RET_DK
            r1, r2 = _rope_halves(p[:, lo:lo + half], p[:, lo + half:lo + RET_DK], cos, sin)
            o_ref[:, part * qd + lo:part * qd + lo + half] = r1.astype(BF16)
            o_ref[:, part * qd + lo + half:part * qd + lo + RET_DK] = r2.astype(BF16)
    for part in range(2):
        lo = 2 * qd + part * vd
        o_ref[:, lo:lo + vd] = _dot(h, w_ref[:, lo:lo + vd]).astype(BF16)


def _proj_att_kernel(x_ref, mod_ref, g_ref, w_ref, qg_ref, kg_ref, cos_ref, sin_ref, o_ref):
    shift, scale, _ = _mod_rows(mod_ref, 3)
    h = _modulated(x_ref[...], g_ref[...], shift, scale).astype(BF16)
    cos, sin = cos_ref[...], sin_ref[...]
    qd = ATT_HEADS * ATT_HD
    kd = ATT_KV_HEADS * ATT_HD

    def norm_rope(p, n_heads, gain, out_lo):
        for hh in range(n_heads):
            lo = hh * ATT_HD
            y = _rms_rows(p[:, lo:lo + ATT_HD]) * gain
            r = y * cos + pltpu.roll(y, ATT_HD // 2, 1) * sin
            o_ref[:, out_lo + lo:out_lo + lo + ATT_HD] = r.astype(BF16)

    norm_rope(_dot(h, w_ref[:, :qd]), ATT_HEADS, qg_ref[...], 0)
    norm_rope(_dot(h, w_ref[:, qd:qd + kd]), ATT_KV_HEADS, kg_ref[...], qd)
    o_ref[:, qd + kd:] = _dot(h, w_ref[:, qd + kd:]).astype(BF16)


def _proj_call(kernel, name, rows, x, mod, layer, gain, w, extras, extra_specs):
    d = x.shape[1]
    n_out = w.shape[1]
    return pl.pallas_call(
        kernel,
        grid=(rows.n_all,),
        in_specs=[rows.row_spec(d), rows.mod_spec(layer, d), pl.BlockSpec((1, d), lambda i: (0, 0)),
                  _resident(w.shape)] + extra_specs,
        out_specs=rows.row_spec(n_out),
        out_shape=jax.ShapeDtypeStruct((rows.n_all * ROW_TILE, n_out), BF16),
        compiler_params=_params(1),
        name=name,
    )(x, mod, gain.reshape(1, d), w, *extras)


def _decay_tables(n, lg_f, lg_b):
    i = lax.broadcasted_iota(jnp.int32, (n, n), 0)
    j = lax.broadcasted_iota(jnp.int32, (n, n), 1)
    diff = (i - j).astype(F32)
    mask = (jnp.where(diff >= 0, jnp.exp(jnp.maximum(diff, 0.0) * lg_f), 0.0)
            + jnp.where(diff <= 0, jnp.exp(jnp.maximum(-diff, 0.0) * lg_b), 0.0))
    pos = lax.broadcasted_iota(jnp.int32, (n, 1), 0).astype(F32)
    xi_f = jnp.exp((pos + 1.0) * lg_f)
    xi_b = jnp.exp((n - pos) * lg_b)
    zeta_f = jnp.exp((n - 1.0 - pos) * lg_f)
    zeta_b = jnp.exp(pos * lg_b)
    return mask, xi_f, xi_b, zeta_f, zeta_b


def _scaled_bf16(x, col):
    return (x.astype(F32) * col).astype(BF16)


def _head_norm_gate(o, g):
    return (_rms_rows(o) * _silu(g.astype(F32))).astype(BF16)


def _ret_kernel(ql_ref, kl_ref, vl_ref, gl_ref, qc_ref, kc_ref, vc_ref, gc_ref, dec_ref,
                ol_ref, oc_ref, sf_ref, sb_ref, run_ref, *, n_chunks, chunk, ctx_len):
    lg_f = -jnp.exp(dec_ref[0, 0:1, 0:1])
    lg_b = -jnp.exp(dec_ref[0, 1:2, 0:1])

    mask, _, _, zeta_f, zeta_b = _decay_tables(ctx_len, lg_f, lg_b)
    q, k, v = qc_ref[...], kc_ref[...], vc_ref[...]
    p = (_dot_nt(q, k) * mask).astype(BF16)
    oc_ref[...] = _head_norm_gate(_dot(p, v), gc_ref[...])
    s_f = _dot_tn(_scaled_bf16(k, zeta_f), v)
    s_b = _dot_tn(_scaled_bf16(k, zeta_b), v)

    mask, xi_f, xi_b, zeta_f, zeta_b = _decay_tables(chunk, lg_f, lg_b)
    decay_f = jnp.exp(chunk * lg_f)
    decay_b = jnp.exp(chunk * lg_b)

    def rows(n):
        return pl.ds(pl.multiple_of(n * chunk, chunk), chunk)

    run_ref[...] = s_f
    sf_ref[0] = s_f.astype(BF16)

    def fwd(n, carry):
        kz = _scaled_bf16(kl_ref[rows(n), :], zeta_f)
        s = run_ref[...] * decay_f + _dot_tn(kz, vl_ref[rows(n), :])
        run_ref[...] = s
        sf_ref[n + 1] = s.astype(BF16)
        return carry

    lax.fori_loop(0, n_chunks - 1, fwd, 0)

    run_ref[...] = s_b
    sb_ref[n_chunks - 1] = s_b.astype(BF16)

    def bwd(t, carry):
        n = n_chunks - 1 - t
        kz = _scaled_bf16(kl_ref[rows(n), :], zeta_b)
        s = run_ref[...] * decay_b + _dot_tn(kz, vl_ref[rows(n), :])
        run_ref[...] = s
        sb_ref[n - 1] = s.astype(BF16)
        return carry

    lax.fori_loop(0, n_chunks - 1, bwd, 0)

    def out(n, carry):
        r = rows(n)
        q, k, v = ql_ref[r, :], kl_ref[r, :], vl_ref[r, :]
        p = (_dot_nt(q, k) * mask).astype(BF16)
        o = _dot(p, v)
        o = o + _dot(_scaled_bf16(q, xi_f), sf_ref[n])
        o = o + _dot(_scaled_bf16(q, xi_b), sb_ref[n])
        ol_ref[r, :] = _head_norm_gate(o, gl_ref[r, :])
        return carry

    lax.fori_loop(0, n_chunks, out, 0)


def _ret_call(batch, seq, ctx_len, proj, dec_tab):
    chunk = min(RET_CHUNK, seq)
    assert seq % chunk == 0
    n_chunks = seq // chunk
    ctx_row0 = batch * seq // ctx_len
    kq = RET_HEADS
    kv = 2 * RET_HEADS * RET_DK // RET_DV
    kg = kv + RET_HEADS
    vd = RET_HEADS * RET_DV
    in_specs = [
        pl.BlockSpec((seq, RET_DK), lambda b, h: (b, h)),
        pl.BlockSpec((seq, RET_DK), lambda b, h: (b, kq + h)),
        pl.BlockSpec((seq, RET_DV), lambda b, h: (b, kv + h)),
        pl.BlockSpec((seq, RET_DV), lambda b, h: (b, kg + h)),
        pl.BlockSpec((ctx_len, RET_DK), lambda b, h: (ctx_row0 + b, h)),
        pl.BlockSpec((ctx_len, RET_DK), lambda b, h: (ctx_row0 + b, kq + h)),
        pl.BlockSpec((ctx_len, RET_DV), lambda b, h: (ctx_row0 + b, kv + h)),
        pl.BlockSpec((ctx_len, RET_DV), lambda b, h: (ctx_row0 + b, kg + h)),
        pl.BlockSpec((1, 8, 128), lambda b, h: (h, 0, 0)),
    ]
    return pl.pallas_call(
        functools.partial(_ret_kernel, n_chunks=n_chunks, chunk=chunk, ctx_len=ctx_len),
        grid=(batch, RET_HEADS),
        in_specs=in_specs,
        out_specs=[pl.BlockSpec((seq, RET_DV), lambda b, h: (b, h)),
                   pl.BlockSpec((ctx_len, RET_DV), lambda b, h: (b, h))],
        out_shape=[jax.ShapeDtypeStruct((batch * seq, vd), BF16),
                   jax.ShapeDtypeStruct((batch * ctx_len, vd), BF16)],
        scratch_shapes=[pltpu.VMEM((n_chunks, RET_DK, RET_DV), BF16),
                        pltpu.VMEM((n_chunks, RET_DK, RET_DV), BF16),
                        pltpu.VMEM((RET_DK, RET_DV), F32)],
        compiler_params=_params(2),
        name="ret",
    )(proj, proj, proj, proj, proj, proj, proj, proj, dec_tab)


def _att_kernel(q_ref, kl_ref, vl_ref, kc_ref, vc_ref, o_ref, s_ref, p_ref, *, n_q, seq):
    qi = pl.program_id(2)
    group = ATT_HEADS // ATT_KV_HEADS
    tq = q_ref.shape[0]
    n_rows = group * tq
    n_keys = s_ref.shape[1]
    exp2_scale = ATT_HD ** -0.5 * 1.4426950408889634

    def attend(col_lo):
        slabs = range(col_lo, n_keys, LANES)
        for g in range(group):
            q = q_ref[:, g * ATT_HD:(g + 1) * ATT_HD]
            if col_lo < seq:
                s_ref[g * tq:(g + 1) * tq, :seq] = _dot_nt(q, kl_ref[...])
            s_ref[g * tq:(g + 1) * tq, seq:] = _dot_nt(q, kc_ref[...])
        for g in range(group):
            denoms = []
            for r0 in range(g * tq, (g + 1) * tq, ATT_ROW_GROUP):
                rs = slice(r0, r0 + ATT_ROW_GROUP)
                m = None
                for c0 in slabs:
                    t = s_ref[rs, c0:c0 + LANES]
                    m = t if m is None else jnp.maximum(m, t)
                m = jnp.max(m, axis=-1, keepdims=True)
                acc = None
                for c0 in slabs:
                    e = jnp.exp2((s_ref[rs, c0:c0 + LANES] - m) * exp2_scale)
                    acc = e if acc is None else acc + e
                    p_ref[rs, c0:c0 + LANES] = e.astype(BF16)
                denoms.append(jnp.sum(acc, axis=-1, keepdims=True))
            rows_g = slice(g * tq, (g + 1) * tq)
            o = _dot(p_ref[rows_g, seq:], vc_ref[...])
            if col_lo < seq:
                o = o + _dot(p_ref[rows_g, :seq], vl_ref[...])
            o = o / jnp.concatenate(denoms, axis=0)
            o_ref[:, g * ATT_HD:(g + 1) * ATT_HD] = o.astype(BF16)

    @pl.when(qi < n_q)
    def _():
        attend(0)

    @pl.when(qi >= n_q)
    def _():
        attend(seq)


def _att_call(batch, seq, ctx_len, proj, ctx_out):
    tq = ATT_Q_TILE
    assert ctx_len == tq and seq % tq == 0
    n_q = seq // tq
    ctx_row0 = batch * seq // ctx_len
    group_w = (ATT_HEADS // ATT_KV_HEADS) * ATT_HD
    k_col = ATT_HEADS
    v_col = ATT_HEADS + ATT_KV_HEADS
    n_rows = batch * seq + (batch * ctx_len if ctx_out else 0)

    def q_map(b, kh, qi):
        return (jnp.where(qi < n_q, b * n_q + qi, ctx_row0 + b), kh)

    return pl.pallas_call(
        functools.partial(_att_kernel, n_q=n_q, seq=seq),
        grid=(batch, ATT_KV_HEADS, n_q + (1 if ctx_out else 0)),
        scratch_shapes=[pltpu.VMEM((group_w // ATT_HD * tq, seq + ctx_len), F32),
                        pltpu.VMEM((group_w // ATT_HD * tq, seq + ctx_len), BF16)],
        in_specs=[
            pl.BlockSpec((tq, group_w), q_map),
            pl.BlockSpec((seq, ATT_HD), lambda b, kh, qi: (b, k_col + kh)),
            pl.BlockSpec((seq, ATT_HD), lambda b, kh, qi: (b, v_col + kh)),
            pl.BlockSpec((ctx_len, ATT_HD), lambda b, kh, qi: (ctx_row0 + b, k_col + kh)),
            pl.BlockSpec((ctx_len, ATT_HD), lambda b, kh, qi: (ctx_row0 + b, v_col + kh)),
        ],
        out_specs=pl.BlockSpec((tq, group_w), q_map),
        out_shape=jax.ShapeDtypeStruct((n_rows, ATT_HEADS * ATT_HD), BF16),
        compiler_params=_params(3),
        name="att",
    )(proj, proj, proj, proj, proj)


def _oproj_kernel(x_ref, mod_ref, w_ref, *rest, n_lat):
    gate = mod_ref[0, 0, 5:6, :]
    if len(rest) == 2:
        a_ref, o_ref = rest
        o_ref[...] = x_ref[...] + gate * _dot(a_ref[...], w_ref[...])
        return
    al_ref, ac_ref, o_ref = rest
    i = pl.program_id(0)

    @pl.when(i < n_lat)
    def _():
        o_ref[...] = x_ref[...] + gate * _dot(al_ref[...], w_ref[...])

    @pl.when(i >= n_lat)
    def _():
        o_ref[...] = x_ref[...] + gate * _dot(ac_ref[...], w_ref[...])


def _oproj_call(rows, n_tiles, x, mod, layer, w, a_lat, a_ctx=None):
    d = x.shape[1]
    k = w.shape[0]
    n_lat = rows.n_lat
    in_specs = [rows.row_spec(d), rows.mod_spec(layer, d), _resident(w.shape)]
    args = [x, mod, w]
    if a_ctx is None:
        in_specs.append(rows.row_spec(k))
        args.append(a_lat)
    else:
        in_specs.append(pl.BlockSpec((ROW_TILE, k), lambda i: (jnp.minimum(i, n_lat - 1), 0)))
        in_specs.append(pl.BlockSpec((ROW_TILE, k), lambda i: (jnp.maximum(i - n_lat, 0), 0)))
        args += [a_lat, a_ctx]
    return pl.pallas_call(
        functools.partial(_oproj_kernel, n_lat=n_lat),
        grid=(n_tiles,),
        in_specs=in_specs,
        out_specs=rows.row_spec(d),
        out_shape=jax.ShapeDtypeStruct((n_tiles * ROW_TILE, d), F32),
        compiler_params=_params(1),
        name="oproj",
    )(*args)


def _with_identity_rows(cos, sin):
    pad = (ROW_TILE, cos.shape[1])
    return (jnp.concatenate([cos, jnp.ones(pad, F32)], axis=0),
            jnp.concatenate([sin, jnp.zeros(pad, F32)], axis=0))


def _seq_rope_tables(seq):
    half = RET_DK // 2
    freqs = ROPE_THETA ** (-jnp.arange(half, dtype=F32) / half)
    ang = jnp.arange(seq, dtype=F32)[:, None] * freqs
    return _with_identity_rows(jnp.cos(ang), jnp.sin(ang))


def _axial_rope_tables(seq):
    quarter = ATT_HD // 4
    tok = jnp.arange(seq)
    r = (tok // GRID_W).astype(F32)
    cl = (tok % GRID_W).astype(F32)
    freqs = ROPE_THETA ** (-jnp.arange(quarter, dtype=F32) / quarter)
    ang = jnp.concatenate([r[:, None] * freqs, cl[:, None] * freqs], axis=-1)
    cos, sin = jnp.cos(ang), jnp.sin(ang)
    return _with_identity_rows(jnp.concatenate([cos, cos], axis=-1),
                               jnp.concatenate([-sin, sin], axis=-1))


def kernel(x, c, ctx, c_ctx, ada_w, ada_b, norm_g, ffn_w1, ffn_w2, ret_w_in, ret_w_out,
           ret_decay_f, ret_decay_b, att_w_qkv, att_w_o, att_q_gain, att_k_gain, final_g):
    batch, seq, d = x.shape
    ctx_len = ctx.shape[1]
    depth = ada_w.shape[0]
    rows = _Rows(batch, seq, ctx_len)

    cc = jnp.concatenate(
        [c, c_ctx[None], jnp.zeros((ADA_ROWS - batch - 1, d), F32)], axis=0)
    mod = _ada_call(cc, ada_w, ada_b).reshape(depth, ADA_ROWS, N_MOD, d)

    w1 = ffn_w1.astype(BF16)
    w2 = ffn_w2.astype(BF16)
    w_in = ret_w_in.astype(BF16)
    w_out = ret_w_out.astype(BF16)
    w_qkv = att_w_qkv.astype(BF16)
    w_o = att_w_o.astype(BF16)

    ret_cos, ret_sin = _seq_rope_tables(seq)
    att_cos, att_sin = _axial_rope_tables(seq)
    ret_rope_specs = [rows.rope_spec(RET_DK // 2)] * 2
    att_rope_specs = [rows.rope_spec(ATT_HD)] * 2
    gain_spec = pl.BlockSpec((1, ATT_HD), lambda i: (0, 0))

    xs = jnp.concatenate([x.reshape(batch * seq, d), ctx.reshape(batch * ctx_len, d)], axis=0)

    for i in range(depth):
        last = i == depth - 1
        j = i // 2
        xs = _ffn_call(rows, rows.n_all, xs, mod, i, 0, norm_g[i, 0], w1[i, 0], w2[i, 0])
        n_tiles = rows.n_lat if last else rows.n_all
        if i % 2 == 0:
            proj = _proj_call(_proj_ret_kernel, "proj_ret", rows, xs, mod, i, norm_g[i, 1],
                              w_in[j], [ret_cos, ret_sin], ret_rope_specs)
            dec_tab = jnp.broadcast_to(
                jnp.stack([ret_decay_f[j], ret_decay_b[j]], axis=1)[:, :, None],
                (RET_HEADS, 2, 128))
            dec_tab = jnp.concatenate([dec_tab, jnp.zeros((RET_HEADS, 6, 128), F32)], axis=1)
            a_lat, a_ctx = _ret_call(batch, seq, ctx_len, proj, dec_tab)
            xs = _oproj_call(rows, n_tiles, xs, mod, i, w_out[j], a_lat,
                             None if last else a_ctx)
        else:
            proj = _proj_call(_proj_att_kernel, "proj_att", rows, xs, mod, i, norm_g[i, 1],
                              w_qkv[j],
                              [att_q_gain[j].reshape(1, ATT_HD), att_k_gain[j].reshape(1, ATT_HD),
                               att_cos, att_sin],
                              [gain_spec, gain_spec] + att_rope_specs)
            a = _att_call(batch, seq, ctx_len, proj, not last)
            xs = _oproj_call(rows, n_tiles, xs, mod, i, w_o[j], a)
        xs = _ffn_call(rows, n_tiles, xs, mod, i, 6, norm_g[i, 2], w1[i, 1], w2[i, 1],
                       final_g if last else None)
    return xs.reshape(batch, seq, d)
```

```python
import functools

import jax
import jax.numpy as jnp
from jax import lax
from jax.experimental import pallas as pl
from jax.experimental.pallas import tpu as pltpu

F32 = jnp.float32
BF16 = jnp.bfloat16

N_MOD = 9
RET_HEADS = 4
RET_DK = 256
RET_DV = 512
ATT_HEADS = 8
ATT_KV_HEADS = 2
ATT_HD = 128
GRID_W = 64
ROPE_THETA = 10000.0
EPS = 1e-6

ROW_TILE = 512
RET_CHUNK = 512
ATT_Q_TILE = 256
ATT_KEY_CHUNK = 512
LANES = 128
MXU_DEPTH = 256
FFN_SPLIT = 2
ADA_ROWS = 16
VMEM_LIMIT = 56 * 1024 * 1024


def _params(n_axes):
    return pltpu.CompilerParams(
        dimension_semantics=("arbitrary",) * n_axes, vmem_limit_bytes=VMEM_LIMIT)


def _resident(stacked, lead):
    tail = stacked.shape[len(lead):]
    index = tuple(lead) + (0,) * len(tail)
    return pl.BlockSpec((None,) * len(lead) + tail, lambda *_: index,
                        pipeline_mode=pl.Buffered(1))


def _silu(x):
    return x * jax.nn.sigmoid(x)


def _rms_rows(x):
    return x * lax.rsqrt(jnp.mean(x * x, axis=-1, keepdims=True) + EPS)


def _modulated(x, gain, shift, scale):
    return (_rms_rows(x) * gain) * (1.0 + scale) + shift


def _dot(a, b):
    return jnp.dot(a, b, preferred_element_type=F32)


def _dot_nt(a, b):
    return lax.dot_general(a, b, (((1,), (1,)), ((), ())), preferred_element_type=F32)


def _dot_tn(a, b):
    return lax.dot_general(a, b, (((0,), (0,)), ((), ())), preferred_element_type=F32)


def _ada_kernel(cc_ref, w_ref, b_ref, o_ref):
    s = _silu(cc_ref[...])
    o_ref[0] = _dot(s.astype(BF16), w_ref[0].astype(BF16)) + b_ref[0]


def _ada_call(cc, ada_w, ada_b):
    depth, d, n = ada_w.shape
    tn = 1024
    return pl.pallas_call(
        _ada_kernel,
        grid=(depth, n // tn),
        in_specs=[
            pl.BlockSpec((ADA_ROWS, d), lambda l, j: (0, 0)),
            pl.BlockSpec((1, d, tn), lambda l, j: (l, 0, j)),
            pl.BlockSpec((1, 1, tn), lambda l, j: (l, 0, j)),
        ],
        out_specs=pl.BlockSpec((1, ADA_ROWS, tn), lambda l, j: (l, 0, j)),
        out_shape=jax.ShapeDtypeStruct((depth, ADA_ROWS, n), F32),
        compiler_params=_params(2),
        name="ada",
    )(cc, ada_w, ada_b.reshape(depth, 1, n))


class _Rows:
    def __init__(self, batch, seq, ctx_len):
        self.batch, self.seq, self.ctx_len = batch, seq, ctx_len
        assert seq % ROW_TILE == 0 and (batch * ctx_len) % ROW_TILE == 0
        self.tiles_per_batch = seq // ROW_TILE
        self.n_lat = batch * seq // ROW_TILE
        self.n_ctx = batch * ctx_len // ROW_TILE
        self.n_all = self.n_lat + self.n_ctx

    def mod_spec(self, layer, d):
        n_lat, tpb, batch = self.n_lat, self.tiles_per_batch, self.batch
        return pl.BlockSpec(
            (1, 1, N_MOD, d),
            lambda i: (layer, jnp.where(i < n_lat, i // tpb, batch), 0, 0))

    def rope_spec(self, width):
        n_lat, tpb = self.n_lat, self.tiles_per_batch
        return pl.BlockSpec((ROW_TILE, width), lambda i: (jnp.where(i < n_lat, i % tpb, tpb), 0))

    @staticmethod
    def row_spec(width):
        return pl.BlockSpec((ROW_TILE, width), lambda i: (i, 0))


def _mod_rows(mod_ref, base):
    return [mod_ref[0, 0, base + k:base + k + 1, :] for k in range(3)]


def _ffn_kernel(*refs, base, ffn_dim, chunks, final, n_lat_split):
    refs = list(refs)
    o_ref = refs.pop()
    fg_ref = refs.pop() if final else None
    if n_lat_split is None:
        x_ref, mod_ref, g_ref, w1_ref, w2_ref = refs
        x = x_ref[...]
    else:
        x_ref, xc_ref, mod_ref, g_ref, w1_ref, w2_ref = refs
        x = jnp.where(pl.program_id(0) < n_lat_split, x_ref[...], xc_ref[...])
    shift, scale, gate = _mod_rows(mod_ref, base)
    h = _modulated(x, g_ref[...], shift, scale).astype(BF16)
    acc = None
    lo = 0
    for width in chunks:
        gt = _dot(h, w1_ref[:, lo:lo + width])
        up = _dot(h, w1_ref[:, ffn_dim + lo:ffn_dim + lo + width])
        a = (_silu(gt) * up).astype(BF16)
        part = _dot(a, w2_ref[lo:lo + width, :])
        acc = part if acc is None else acc + part
        lo += width
    y = x + (0.5 * gate) * acc
    if final:
        y = _rms_rows(y) * fg_ref[...]
    o_ref[...] = y


def _ffn_chunks(ffn_dim):
    assert ffn_dim % MXU_DEPTH == 0
    n_tiles = ffn_dim // MXU_DEPTH
    first = (n_tiles // FFN_SPLIT) * MXU_DEPTH
    sizes = [first] * (FFN_SPLIT - 1) + [ffn_dim - first * (FFN_SPLIT - 1)]
    return tuple(s for s in sizes if s)


def _ffn_call(rows, n_tiles, x, mod, layer, base, gain, w1, w2, which, final_g=None, x_ctx=None):
    d = x.shape[1]
    ffn_dim = w2.shape[-2]
    chunks = _ffn_chunks(ffn_dim)
    final = final_g is not None
    n_lat = rows.n_lat
    if x_ctx is None:
        in_specs, args = [rows.row_spec(d)], [x]
    else:
        in_specs = [pl.BlockSpec((ROW_TILE, d), lambda i: (jnp.minimum(i, n_lat - 1), 0)),
                    pl.BlockSpec((ROW_TILE, d), lambda i: (jnp.maximum(i - n_lat, 0), 0))]
        args = [x, x_ctx]
    in_specs += [rows.mod_spec(layer, d), pl.BlockSpec((1, d), lambda i: (0, 0)),
                 _resident(w1, (layer, which)), _resident(w2, (layer, which))]
    args += [mod, gain.reshape(1, d), w1, w2]
    if final:
        in_specs.append(pl.BlockSpec((1, d), lambda i: (0, 0)))
        args.append(final_g.reshape(1, d))
    return pl.pallas_call(
        functools.partial(_ffn_kernel, base=base, ffn_dim=ffn_dim, chunks=chunks, final=final,
                          n_lat_split=None if x_ctx is None else n_lat),
        grid=(n_tiles,),
        in_specs=in_specs,
        out_specs=rows.row_spec(d),
        out_shape=jax.ShapeDtypeStruct((n_tiles * ROW_TILE, d), F32),
        compiler_params=_params(1),
        name="ffn",
    )(*args)


def _rope_halves(x1, x2, cos, sin):
    return x1 * cos - x2 * sin, x2 * cos + x1 * sin


def _proj_ret_kernel(x_ref, mod_ref, g_ref, w_ref, cos_ref, sin_ref, o_ref):
    shift, scale, _ = _mod_rows(mod_ref, 3)
    h = _modulated(x_ref[...], g_ref[...], shift, scale).astype(BF16)
    cos, sin = cos_ref[...], sin_ref[...]
    qd = RET_HEADS * RET_DK
    vd = RET_HEADS * RET_DV
    half = RET_DK // 2
    for part, mult in ((0, None), (1, RET_DK ** -0.5)):
        p = _dot(h, w_ref[:, part * qd:(part + 1) * qd])
        if mult is not None:
            p = p * mult
        for hh in range(RET_HEADS):
            lo = hh * RET_DK
            r1, r2 = _rope_halves(p[:, lo:lo + half], p[:, lo + half:lo + RET_DK], cos, sin)
            o_ref[:, part * qd + lo:part * qd + lo + half] = r1.astype(BF16)
            o_ref[:, part * qd + lo + half:part * qd + lo + RET_DK] = r2.astype(BF16)
    for part in range(2):
        lo = 2 * qd + part * vd
        o_ref[:, lo:lo + vd] = _dot(h, w_ref[:, lo:lo + vd]).astype(BF16)


def _proj_att_kernel(x_ref, mod_ref, g_ref, w_ref, qg_ref, kg_ref, cos_ref, sin_ref, o_ref):
    shift, scale, _ = _mod_rows(mod_ref, 3)
    h = _modulated(x_ref[...], g_ref[...], shift, scale).astype(BF16)
    cos, sin = cos_ref[...], sin_ref[...]
    qd = ATT_HEADS * ATT_HD
    kd = ATT_KV_HEADS * ATT_HD

    def norm_rope(p, n_heads, gain, out_lo):
        for hh in range(n_heads):
            lo = hh * ATT_HD
            y = _rms_rows(p[:, lo:lo + ATT_HD]) * gain
            r = y * cos + pltpu.roll(y, ATT_HD // 2, 1) * sin
            o_ref[:, out_lo + lo:out_lo + lo + ATT_HD] = r.astype(BF16)

    norm_rope(_dot(h, w_ref[:, :qd]), ATT_HEADS, qg_ref[...], 0)
    norm_rope(_dot(h, w_ref[:, qd:qd + kd]), ATT_KV_HEADS, kg_ref[...], qd)
    o_ref[:, qd + kd:] = _dot(h, w_ref[:, qd + kd:]).astype(BF16)


def _proj_call(kernel, name, rows, x, mod, layer, gain, w, which, extras, extra_specs):
    d = x.shape[1]
    n_out = w.shape[-1]
    return pl.pallas_call(
        kernel,
        grid=(rows.n_all,),
        in_specs=[rows.row_spec(d), rows.mod_spec(layer, d), pl.BlockSpec((1, d), lambda i: (0, 0)),
                  _resident(w, (which,))] + extra_specs,
        out_specs=rows.row_spec(n_out),
        out_shape=jax.ShapeDtypeStruct((rows.n_all * ROW_TILE, n_out), BF16),
        compiler_params=_params(1),
        name=name,
    )(x, mod, gain.reshape(1, d), w, *extras)


def _decay_tables(n, lg_f, lg_b):
    i = lax.broadcasted_iota(jnp.int32, (n, n), 0)
    j = lax.broadcasted_iota(jnp.int32, (n, n), 1)
    diff = (i - j).astype(F32)
    mask = (jnp.where(diff >= 0, jnp.exp(jnp.maximum(diff, 0.0) * lg_f), 0.0)
            + jnp.where(diff <= 0, jnp.exp(jnp.maximum(-diff, 0.0) * lg_b), 0.0))
    pos = lax.broadcasted_iota(jnp.int32, (n, 1), 0).astype(F32)
    xi_f = jnp.exp((pos + 1.0) * lg_f)
    xi_b = jnp.exp((n - pos) * lg_b)
    zeta_f = jnp.exp((n - 1.0 - pos) * lg_f)
    zeta_b = jnp.exp(pos * lg_b)
    return mask, xi_f, xi_b, zeta_f, zeta_b


def _scaled_bf16(x, col):
    return (x.astype(F32) * col).astype(BF16)


def _head_norm_gate(o, g):
    return (_rms_rows(o) * _silu(g.astype(F32))).astype(BF16)


def _ret_kernel(ql_ref, kl_ref, vl_ref, gl_ref, qc_ref, kc_ref, vc_ref, gc_ref, dec_ref,
                ol_ref, oc_ref, sf_ref, sb_ref, run_ref, *, n_chunks, chunk, ctx_len):
    lg_f = -jnp.exp(dec_ref[0, 0:1, 0:1])
    lg_b = -jnp.exp(dec_ref[0, 1:2, 0:1])

    mask, _, _, zeta_f, zeta_b = _decay_tables(ctx_len, lg_f, lg_b)
    q, k, v = qc_ref[...], kc_ref[...], vc_ref[...]
    p = (_dot_nt(q, k) * mask).astype(BF16)
    oc_ref[...] = _head_norm_gate(_dot(p, v), gc_ref[...])
    s_f = _dot_tn(_scaled_bf16(k, zeta_f), v)
    s_b = _dot_tn(_scaled_bf16(k, zeta_b), v)

    mask, xi_f, xi_b, zeta_f, zeta_b = _decay_tables(chunk, lg_f, lg_b)
    decay_f = jnp.exp(chunk * lg_f)
    decay_b = jnp.exp(chunk * lg_b)

    def rows(n):
        return pl.ds(pl.multiple_of(n * chunk, chunk), chunk)

    run_ref[...] = s_f
    sf_ref[0] = s_f.astype(BF16)

    def fwd(n, carry):
        kz = _scaled_bf16(kl_ref[rows(n), :], zeta_f)
        s = run_ref[...] * decay_f + _dot_tn(kz, vl_ref[rows(n), :])
        run_ref[...] = s
        sf_ref[n + 1] = s.astype(BF16)
        return carry

    lax.fori_loop(0, n_chunks - 1, fwd, 0)

    run_ref[...] = s_b
    sb_ref[n_chunks - 1] = s_b.astype(BF16)

    def bwd(t, carry):
        n = n_chunks - 1 - t
        kz = _scaled_bf16(kl_ref[rows(n), :], zeta_b)
        s = run_ref[...] * decay_b + _dot_tn(kz, vl_ref[rows(n), :])
        run_ref[...] = s
        sb_ref[n - 1] = s.astype(BF16)
        return carry

    lax.fori_loop(0, n_chunks - 1, bwd, 0)

    def out(n, carry):
        r = rows(n)
        q, k, v = ql_ref[r, :], kl_ref[r, :], vl_ref[r, :]
        p = (_dot_nt(q, k) * mask).astype(BF16)
        o = _dot(p, v)
        o = o + _dot(_scaled_bf16(q, xi_f), sf_ref[n])
        o = o + _dot(_scaled_bf16(q, xi_b), sb_ref[n])
        ol_ref[r, :] = _head_norm_gate(o, gl_ref[r, :])
        return carry

    lax.fori_loop(0, n_chunks, out, 0)


def _ret_call(batch, seq, ctx_len, proj, dec_tab):
    chunk = min(RET_CHUNK, seq)
    assert seq % chunk == 0
    n_chunks = seq // chunk
    ctx_row0 = batch * seq // ctx_len
    kq = RET_HEADS
    kv = 2 * RET_HEADS * RET_DK // RET_DV
    kg = kv + RET_HEADS
    vd = RET_HEADS * RET_DV
    in_specs = [
        pl.BlockSpec((seq, RET_DK), lambda b, h: (b, h)),
        pl.BlockSpec((seq, RET_DK), lambda b, h: (b, kq + h)),
        pl.BlockSpec((seq, RET_DV), lambda b, h: (b, kv + h)),
        pl.BlockSpec((seq, RET_DV), lambda b, h: (b, kg + h)),
        pl.BlockSpec((ctx_len, RET_DK), lambda b, h: (ctx_row0 + b, h)),
        pl.BlockSpec((ctx_len, RET_DK), lambda b, h: (ctx_row0 + b, kq + h)),
        pl.BlockSpec((ctx_len, RET_DV), lambda b, h: (ctx_row0 + b, kv + h)),
        pl.BlockSpec((ctx_len, RET_DV), lambda b, h: (ctx_row0 + b, kg + h)),
        pl.BlockSpec((1, 8, 128), lambda b, h: (h, 0, 0)),
    ]
    return pl.pallas_call(
        functools.partial(_ret_kernel, n_chunks=n_chunks, chunk=chunk, ctx_len=ctx_len),
        grid=(batch, RET_HEADS),
        in_specs=in_specs,
        out_specs=[pl.BlockSpec((seq, RET_DV), lambda b, h: (b, h)),
                   pl.BlockSpec((ctx_len, RET_DV), lambda b, h: (b, h))],
        out_shape=[jax.ShapeDtypeStruct((batch * seq, vd), BF16),
                   jax.ShapeDtypeStruct((batch * ctx_len, vd), BF16)],
        scratch_shapes=[pltpu.VMEM((n_chunks, RET_DK, RET_DV), BF16),
                        pltpu.VMEM((n_chunks, RET_DK, RET_DV), BF16),
                        pltpu.VMEM((RET_DK, RET_DV), F32)],
        compiler_params=_params(2),
        name="ret",
    )(proj, proj, proj, proj, proj, proj, proj, proj, dec_tab)


def _att_kernel(q_ref, kl_ref, vl_ref, kc_ref, vc_ref, o_ref, s_ref, p_ref, *, n_q, seq):
    qi = pl.program_id(2)
    group = ATT_HEADS // ATT_KV_HEADS
    tq = q_ref.shape[0]
    n_rows = group * tq
    n_keys = s_ref.shape[1]
    exp2_scale = ATT_HD ** -0.5 * 1.4426950408889634

    def attend(col_lo):
        slabs = range(col_lo, n_keys, LANES)
        for g in range(group):
            q = q_ref[:, g * ATT_HD:(g + 1) * ATT_HD]
            if col_lo < seq:
                s_ref[g * tq:(g + 1) * tq, :seq] = _dot_nt(q, kl_ref[...])
            s_ref[g * tq:(g + 1) * tq, seq:] = _dot_nt(q, kc_ref[...])
        for g in range(group):
            rs = slice(g * tq, (g + 1) * tq)
            m = None
            for c0 in slabs:
                t = s_ref[rs, c0:c0 + LANES]
                m = t if m is None else jnp.maximum(m, t)
            m = jnp.max(m, axis=-1, keepdims=True)
            acc = None
            o = None
            for k0 in range(col_lo, n_keys, ATT_KEY_CHUNK):
                k1 = min(k0 + ATT_KEY_CHUNK, seq if k0 < seq else n_keys)
                for c0 in range(k0, k1, LANES):
                    e = jnp.exp2((s_ref[rs, c0:c0 + LANES] - m) * exp2_scale)
                    acc = e if acc is None else acc + e
                    p_ref[rs, c0:c0 + LANES] = e.astype(BF16)
                v = vl_ref[k0:k1, :] if k0 < seq else vc_ref[k0 - seq:k1 - seq, :]
                part = _dot(p_ref[rs, k0:k1], v)
                o = part if o is None else o + part
            o = jnp.where(pl.program_id(1) == 0, o[:, :ATT_HD], o[:, ATT_HD:])
            o = o / jnp.sum(acc, axis=-1, keepdims=True)
            o_ref[:, g * ATT_HD:(g + 1) * ATT_HD] = o.astype(BF16)

    @pl.when(qi < n_q)
    def _():
        attend(0)

    @pl.when(qi >= n_q)
    def _():
        attend(seq)


def _att_call(batch, seq, ctx_len, proj, ctx_out):
    tq = ATT_Q_TILE
    assert ctx_len == tq and seq % tq == 0
    n_q = seq // tq
    ctx_row0 = batch * seq // ctx_len
    group_w = (ATT_HEADS // ATT_KV_HEADS) * ATT_HD
    k_col = ATT_HEADS
    v_col = ATT_HEADS + ATT_KV_HEADS
    n_rows = batch * seq + (batch * ctx_len if ctx_out else 0)

    def q_map(b, kh, qi):
        return (jnp.where(qi < n_q, b * n_q + qi, ctx_row0 + b), kh)

    return pl.pallas_call(
        functools.partial(_att_kernel, n_q=n_q, seq=seq),
        grid=(batch, ATT_KV_HEADS, n_q + (1 if ctx_out else 0)),
        scratch_shapes=[pltpu.VMEM((group_w // ATT_HD * tq, seq + ctx_len), F32),
                        pltpu.VMEM((group_w // ATT_HD * tq, seq + ctx_len), BF16)],
        in_specs=[
            pl.BlockSpec((tq, group_w), q_map),
            pl.BlockSpec((seq, ATT_HD), lambda b, kh, qi: (b, k_col + kh)),
            pl.BlockSpec((seq, 2 * ATT_HD), lambda b, kh, qi: (b, v_col // 2)),
            pl.BlockSpec((ctx_len, ATT_HD), lambda b, kh, qi: (ctx_row0 + b, k_col + kh)),
            pl.BlockSpec((ctx_len, 2 * ATT_HD), lambda b, kh, qi: (ctx_row0 + b, v_col // 2)),
        ],
        out_specs=pl.BlockSpec((tq, group_w), q_map),
        out_shape=jax.ShapeDtypeStruct((n_rows, ATT_HEADS * ATT_HD), BF16),
        compiler_params=_params(3),
        name="att",
    )(proj, proj, proj, proj, proj)


def _oproj_kernel(x_ref, mod_ref, w_ref, *rest, n_lat):
    gate = mod_ref[0, 0, 5:6, :]
    if len(rest) == 2:
        a_ref, o_ref = rest
        o_ref[...] = x_ref[...] + gate * _dot(a_ref[...], w_ref[...])
        return
    al_ref, ac_ref, o_ref = rest
    i = pl.program_id(0)

    @pl.when(i < n_lat)
    def _():
        o_ref[...] = x_ref[...] + gate * _dot(al_ref[...], w_ref[...])

    @pl.when(i >= n_lat)
    def _():
        o_ref[...] = x_ref[...] + gate * _dot(ac_ref[...], w_ref[...])


def _oproj_call(rows, n_tiles, x, mod, layer, w, which, a_lat, a_ctx=None):
    d = x.shape[1]
    k = w.shape[-2]
    n_lat = rows.n_lat
    in_specs = [rows.row_spec(d), rows.mod_spec(layer, d), _resident(w, (which,))]
    args = [x, mod, w]
    if a_ctx is None:
        in_specs.append(rows.row_spec(k))
        args.append(a_lat)
    else:
        in_specs.append(pl.BlockSpec((ROW_TILE, k), lambda i: (jnp.minimum(i, n_lat - 1), 0)))
        in_specs.append(pl.BlockSpec((ROW_TILE, k), lambda i: (jnp.maximum(i - n_lat, 0), 0)))
        args += [a_lat, a_ctx]
    return pl.pallas_call(
        functools.partial(_oproj_kernel, n_lat=n_lat),
        grid=(n_tiles,),
        in_specs=in_specs,
        out_specs=rows.row_spec(d),
        out_shape=jax.ShapeDtypeStruct((n_tiles * ROW_TILE, d), F32),
        compiler_params=_params(1),
        name="oproj",
    )(*args)


def _with_identity_rows(cos, sin):
    pad = (ROW_TILE, cos.shape[1])
    return (jnp.concatenate([cos, jnp.ones(pad, F32)], axis=0),
            jnp.concatenate([sin, jnp.zeros(pad, F32)], axis=0))


def _seq_rope_tables(seq):
    half = RET_DK // 2
    freqs = ROPE_THETA ** (-jnp.arange(half, dtype=F32) / half)
    ang = jnp.arange(seq, dtype=F32)[:, None] * freqs
    return _with_identity_rows(jnp.cos(ang), jnp.sin(ang))


def _axial_rope_tables(seq):
    quarter = ATT_HD // 4
    tok = jnp.arange(seq)
    r = (tok // GRID_W).astype(F32)
    cl = (tok % GRID_W).astype(F32)
    freqs = ROPE_THETA ** (-jnp.arange(quarter, dtype=F32) / quarter)
    ang = jnp.concatenate([r[:, None] * freqs, cl[:, None] * freqs], axis=-1)
    cos, sin = jnp.cos(ang), jnp.sin(ang)
    return _with_identity_rows(jnp.concatenate([cos, cos], axis=-1),
                               jnp.concatenate([-sin, sin], axis=-1))


def kernel(x, c, ctx, c_ctx, ada_w, ada_b, norm_g, ffn_w1, ffn_w2, ret_w_in, ret_w_out,
           ret_decay_f, ret_decay_b, att_w_qkv, att_w_o, att_q_gain, att_k_gain, final_g):
    batch, seq, d = x.shape
    ctx_len = ctx.shape[1]
    depth = ada_w.shape[0]
    rows = _Rows(batch, seq, ctx_len)

    cc = jnp.concatenate(
        [c, c_ctx[None], jnp.zeros((ADA_ROWS - batch - 1, d), F32)], axis=0)
    mod = _ada_call(cc, ada_w, ada_b).reshape(depth, ADA_ROWS, N_MOD, d)

    w1 = ffn_w1.astype(BF16)
    w2 = ffn_w2.astype(BF16)
    w_in = ret_w_in.astype(BF16)
    w_out = ret_w_out.astype(BF16)
    w_qkv = att_w_qkv.astype(BF16)
    w_o = att_w_o.astype(BF16)

    ret_cos, ret_sin = _seq_rope_tables(seq)
    att_cos, att_sin = _axial_rope_tables(seq)
    ret_rope_specs = [rows.rope_spec(RET_DK // 2)] * 2
    att_rope_specs = [rows.rope_spec(ATT_HD)] * 2
    gain_spec = pl.BlockSpec((1, ATT_HD), lambda i: (0, 0))

    xs = x.reshape(batch * seq, d)
    xs_ctx = ctx.reshape(batch * ctx_len, d)

    for i in range(depth):
        last = i == depth - 1
        j = i // 2
        xs = _ffn_call(rows, rows.n_all, xs, mod, i, 0, norm_g[i, 0], w1, w2, 0,
                       x_ctx=xs_ctx if i == 0 else None)
        n_tiles = rows.n_lat if last else rows.n_all
        if i % 2 == 0:
            proj = _proj_call(_proj_ret_kernel, "proj_ret", rows, xs, mod, i, norm_g[i, 1],
                              w_in, j, [ret_cos, ret_sin], ret_rope_specs)
            dec_tab = jnp.broadcast_to(
                jnp.stack([ret_decay_f[j], ret_decay_b[j]], axis=1)[:, :, None],
                (RET_HEADS, 2, 128))
            dec_tab = jnp.concatenate([dec_tab, jnp.zeros((RET_HEADS, 6, 128), F32)], axis=1)
            a_lat, a_ctx = _ret_call(batch, seq, ctx_len, proj, dec_tab)
            xs = _oproj_call(rows, n_tiles, xs, mod, i, w_out, j, a_lat,
                             None if last else a_ctx)
        else:
            proj = _proj_call(_proj_att_kernel, "proj_att", rows, xs, mod, i, norm_g[i, 1],
                              w_qkv, j,
                              [att_q_gain[j].reshape(1, ATT_HD), att_k_gain[j].reshape(1, ATT_HD),
                               att_cos, att_sin],
                              [gain_spec, gain_spec] + att_rope_specs)
            a = _att_call(batch, seq, ctx_len, proj, not last)
            xs = _oproj_call(rows, n_tiles, xs, mod, i, w_o, j, a)
        xs = _ffn_call(rows, n_tiles, xs, mod, i, 6, norm_g[i, 2], w1, w2, 1,
                       final_g if last else None)
    return xs.reshape(batch, seq, d)
```

```python
import functools

import jax
import jax.numpy as jnp
from jax import lax
from jax.experimental import pallas as pl
from jax.experimental.pallas import tpu as pltpu

F32 = jnp.float32
BF16 = jnp.bfloat16

N_MOD = 9
RET_HEADS = 4
RET_DK = 256
RET_DV = 512
ATT_HEADS = 8
ATT_KV_HEADS = 2
ATT_HD = 128
GRID_W = 64
ROPE_THETA = 10000.0
EPS = 1e-6

ROW_TILE = 512
RET_CHUNK = 512
ATT_Q_TILE = 256
ATT_KEY_CHUNK = 512
LANES = 128
MXU_DEPTH = 256
FFN_SPLIT = 2
ADA_ROWS = 16
VMEM_LIMIT = 56 * 1024 * 1024


def _params(n_axes):
    return pltpu.CompilerParams(
        dimension_semantics=("arbitrary",) * n_axes, vmem_limit_bytes=VMEM_LIMIT)


def _resident(stacked, lead):
    tail = stacked.shape[len(lead):]
    index = tuple(lead) + (0,) * len(tail)
    return pl.BlockSpec((None,) * len(lead) + tail, lambda *_: index,
                        pipeline_mode=pl.Buffered(1))


def _silu(x):
    return x * jax.nn.sigmoid(x)


def _rms_rows(x):
    return x * lax.rsqrt(jnp.mean(x * x, axis=-1, keepdims=True) + EPS)


def _modulated(x, gain, shift, scale):
    return (_rms_rows(x) * gain) * (1.0 + scale) + shift


def _dot(a, b):
    return jnp.dot(a, b, preferred_element_type=F32)


def _dot_nt(a, b):
    return lax.dot_general(a, b, (((1,), (1,)), ((), ())), preferred_element_type=F32)


def _dot_tn(a, b):
    return lax.dot_general(a, b, (((0,), (0,)), ((), ())), preferred_element_type=F32)


def _ada_kernel(cc_ref, w_ref, b_ref, o_ref):
    s = _silu(cc_ref[...])
    o_ref[0] = _dot(s.astype(BF16), w_ref[0].astype(BF16)) + b_ref[0]


def _ada_call(cc, ada_w, ada_b):
    depth, d, n = ada_w.shape
    tn = 1024
    return pl.pallas_call(
        _ada_kernel,
        grid=(depth, n // tn),
        in_specs=[
            pl.BlockSpec((ADA_ROWS, d), lambda l, j: (0, 0)),
            pl.BlockSpec((1, d, tn), lambda l, j: (l, 0, j)),
            pl.BlockSpec((1, 1, tn), lambda l, j: (l, 0, j)),
        ],
        out_specs=pl.BlockSpec((1, ADA_ROWS, tn), lambda l, j: (l, 0, j)),
        out_shape=jax.ShapeDtypeStruct((depth, ADA_ROWS, n), F32),
        compiler_params=_params(2),
        name="ada",
    )(cc, ada_w, ada_b.reshape(depth, 1, n))


class _Rows:
    def __init__(self, batch, seq, ctx_len):
        self.batch, self.seq, self.ctx_len = batch, seq, ctx_len
        assert seq % ROW_TILE == 0 and (batch * ctx_len) % ROW_TILE == 0
        self.tiles_per_batch = seq // ROW_TILE
        self.n_lat = batch * seq // ROW_TILE
        self.n_ctx = batch * ctx_len // ROW_TILE
        self.n_all = self.n_lat + self.n_ctx

    def mod_spec(self, layer, d):
        n_lat, tpb, batch = self.n_lat, self.tiles_per_batch, self.batch
        return pl.BlockSpec(
            (1, 1, N_MOD, d),
            lambda i: (layer, jnp.where(i < n_lat, i // tpb, batch), 0, 0))

    def rope_spec(self, width):
        n_lat, tpb = self.n_lat, self.tiles_per_batch
        return pl.BlockSpec((ROW_TILE, width), lambda i: (jnp.where(i < n_lat, i % tpb, tpb), 0))

    @staticmethod
    def row_spec(width):
        return pl.BlockSpec((ROW_TILE, width), lambda i: (i, 0))


def _mod_rows(mod_ref, base):
    return [mod_ref[0, 0, base + k:base + k + 1, :] for k in range(3)]


def _ffn_kernel(*refs, base, ffn_dim, chunks, final, n_lat_split):
    refs = list(refs)
    o_ref = refs.pop()
    fg_ref = refs.pop() if final else None
    if n_lat_split is None:
        x_ref, mod_ref, g_ref, w1_ref, w2_ref = refs
        x = x_ref[...]
    else:
        x_ref, xc_ref, mod_ref, g_ref, w1_ref, w2_ref = refs
        x = jnp.where(pl.program_id(0) < n_lat_split, x_ref[...], xc_ref[...])
    shift, scale, gate = _mod_rows(mod_ref, base)
    h = _modulated(x, g_ref[...], shift, scale).astype(BF16)
    acc = None
    lo = 0
    for width in chunks:
        gt = _dot(h, w1_ref[:, lo:lo + width])
        up = _dot(h, w1_ref[:, ffn_dim + lo:ffn_dim + lo + width])
        a = (_silu(gt) * up).astype(BF16)
        part = _dot(a, w2_ref[lo:lo + width, :])
        acc = part if acc is None else acc + part
        lo += width
    y = x + (0.5 * gate) * acc
    if final:
        y = _rms_rows(y) * fg_ref[...]
    o_ref[...] = y


def _ffn_chunks(ffn_dim):
    assert ffn_dim % MXU_DEPTH == 0
    n_tiles = ffn_dim // MXU_DEPTH
    first = (n_tiles // FFN_SPLIT) * MXU_DEPTH
    sizes = [first] * (FFN_SPLIT - 1) + [ffn_dim - first * (FFN_SPLIT - 1)]
    return tuple(s for s in sizes if s)


def _ffn_call(rows, n_tiles, x, mod, layer, base, gain, w1, w2, which, final_g=None, x_ctx=None):
    d = x.shape[1]
    ffn_dim = w2.shape[-2]
    chunks = _ffn_chunks(ffn_dim)
    final = final_g is not None
    n_lat = rows.n_lat
    if x_ctx is None:
        in_specs, args = [rows.row_spec(d)], [x]
    else:
        in_specs = [pl.BlockSpec((ROW_TILE, d), lambda i: (jnp.minimum(i, n_lat - 1), 0)),
                    pl.BlockSpec((ROW_TILE, d), lambda i: (jnp.maximum(i - n_lat, 0), 0))]
        args = [x, x_ctx]
    in_specs += [rows.mod_spec(layer, d), pl.BlockSpec((1, d), lambda i: (0, 0)),
                 _resident(w1, (layer, which)), _resident(w2, (layer, which))]
    args += [mod, gain.reshape(1, d), w1, w2]
    if final:
        in_specs.append(pl.BlockSpec((1, d), lambda i: (0, 0)))
        args.append(final_g.reshape(1, d))
    return pl.pallas_call(
        functools.partial(_ffn_kernel, base=base, ffn_dim=ffn_dim, chunks=chunks, final=final,
                          n_lat_split=None if x_ctx is None else n_lat),
        grid=(n_tiles,),
        in_specs=in_specs,
        out_specs=rows.row_spec(d),
        out_shape=jax.ShapeDtypeStruct((n_tiles * ROW_TILE, d), F32),
        compiler_params=_params(1),
        name="ffn",
    )(*args)


def _rope_halves(x1, x2, cos, sin):
    return x1 * cos - x2 * sin, x2 * cos + x1 * sin


def _proj_ret_kernel(x_ref, mod_ref, g_ref, w_ref, cos_ref, sin_ref, o_ref):
    shift, scale, _ = _mod_rows(mod_ref, 3)
    h = _modulated(x_ref[...], g_ref[...], shift, scale).astype(BF16)
    cos, sin = cos_ref[...], sin_ref[...]
    qd = RET_HEADS * RET_DK
    vd = RET_HEADS * RET_DV
    half = RET_DK // 2
    for part, mult in ((0, None), (1, RET_DK ** -0.5)):
        p = _dot(h, w_ref[:, part * qd:(part + 1) * qd])
        if mult is not None:
            p = p * mult
        for hh in range(RET_HEADS):
            lo = hh * RET_DK
            r1, r2 = _rope_halves(p[:, lo:lo + half], p[:, lo + half:lo + RET_DK], cos, sin)
            o_ref[:, part * qd + lo:part * qd + lo + half] = r1.astype(BF16)
            o_ref[:, part * qd + lo + half:part * qd + lo + RET_DK] = r2.astype(BF16)
    for part in range(2):
        lo = 2 * qd + part * vd
        o_ref[:, lo:lo + vd] = _dot(h, w_ref[:, lo:lo + vd]).astype(BF16)


def _proj_att_kernel(x_ref, mod_ref, g_ref, w_ref, qg_ref, kg_ref, cos_ref, sin_ref, o_ref):
    shift, scale, _ = _mod_rows(mod_ref, 3)
    h = _modulated(x_ref[...], g_ref[...], shift, scale).astype(BF16)
    cos, sin = cos_ref[...], sin_ref[...]
    qd = ATT_HEADS * ATT_HD
    kd = ATT_KV_HEADS * ATT_HD

    def norm_rope(p, n_heads, gain, out_lo):
        for hh in range(n_heads):
            lo = hh * ATT_HD
            y = _rms_rows(p[:, lo:lo + ATT_HD]) * gain
            r = y * cos + pltpu.roll(y, ATT_HD // 2, 1) * sin
            o_ref[:, out_lo + lo:out_lo + lo + ATT_HD] = r.astype(BF16)

    norm_rope(_dot(h, w_ref[:, :qd]), ATT_HEADS, qg_ref[...], 0)
    norm_rope(_dot(h, w_ref[:, qd:qd + kd]), ATT_KV_HEADS, kg_ref[...], qd)
    o_ref[:, qd + kd:] = _dot(h, w_ref[:, qd + kd:]).astype(BF16)


def _proj_call(kernel, name, rows, x, mod, layer, gain, w, which, extras, extra_specs):
    d = x.shape[1]
    n_out = w.shape[-1]
    return pl.pallas_call(
        kernel,
        grid=(rows.n_all,),
        in_specs=[rows.row_spec(d), rows.mod_spec(layer, d), pl.BlockSpec((1, d), lambda i: (0, 0)),
                  _resident(w, (which,))] + extra_specs,
        out_specs=rows.row_spec(n_out),
        out_shape=jax.ShapeDtypeStruct((rows.n_all * ROW_TILE, n_out), BF16),
        compiler_params=_params(1),
        name=name,
    )(x, mod, gain.reshape(1, d), w, *extras)


def _decay_tables(n, lg_f, lg_b):
    i = lax.broadcasted_iota(jnp.int32, (n, n), 0)
    j = lax.broadcasted_iota(jnp.int32, (n, n), 1)
    diff = (i - j).astype(F32)
    mask = (jnp.where(diff >= 0, jnp.exp(jnp.maximum(diff, 0.0) * lg_f), 0.0)
            + jnp.where(diff <= 0, jnp.exp(jnp.maximum(-diff, 0.0) * lg_b), 0.0))
    pos = lax.broadcasted_iota(jnp.int32, (n, 1), 0).astype(F32)
    xi_f = jnp.exp((pos + 1.0) * lg_f)
    xi_b = jnp.exp((n - pos) * lg_b)
    zeta_f = jnp.exp((n - 1.0 - pos) * lg_f)
    zeta_b = jnp.exp(pos * lg_b)
    return mask, xi_f, xi_b, zeta_f, zeta_b


def _scaled_bf16(x, col):
    return (x.astype(F32) * col).astype(BF16)


def _head_norm_gate(o, g):
    return (_rms_rows(o) * _silu(g.astype(F32))).astype(BF16)


def _ret_kernel(ql_ref, kl_ref, vl_ref, gl_ref, qc_ref, kc_ref, vc_ref, gc_ref, dec_ref,
                ol_ref, oc_ref, sf_ref, sb_ref, run_ref, *, n_chunks, chunk, ctx_len):
    lg_f = -jnp.exp(dec_ref[0, 0:1, 0:1])
    lg_b = -jnp.exp(dec_ref[0, 1:2, 0:1])

    mask, _, _, zeta_f, zeta_b = _decay_tables(ctx_len, lg_f, lg_b)
    q, k, v = qc_ref[...], kc_ref[...], vc_ref[...]
    p = (_dot_nt(q, k) * mask).astype(BF16)
    oc_ref[...] = _head_norm_gate(_dot(p, v), gc_ref[...])
    s_f = _dot_tn(_scaled_bf16(k, zeta_f), v)
    s_b = _dot_tn(_scaled_bf16(k, zeta_b), v)

    mask, xi_f, xi_b, zeta_f, zeta_b = _decay_tables(chunk, lg_f, lg_b)
    decay_f = jnp.exp(chunk * lg_f)
    decay_b = jnp.exp(chunk * lg_b)

    def rows(n):
        return pl.ds(pl.multiple_of(n * chunk, chunk), chunk)

    run_ref[...] = s_f
    sf_ref[0] = s_f.astype(BF16)

    def fwd(n, carry):
        kz = _scaled_bf16(kl_ref[rows(n), :], zeta_f)
        s = run_ref[...] * decay_f + _dot_tn(kz, vl_ref[rows(n), :])
        run_ref[...] = s
        sf_ref[n + 1] = s.astype(BF16)
        return carry

    lax.fori_loop(0, n_chunks - 1, fwd, 0)

    run_ref[...] = s_b
    sb_ref[n_chunks - 1] = s_b.astype(BF16)

    def bwd(t, carry):
        n = n_chunks - 1 - t
        kz = _scaled_bf16(kl_ref[rows(n), :], zeta_b)
        s = run_ref[...] * decay_b + _dot_tn(kz, vl_ref[rows(n), :])
        run_ref[...] = s
        sb_ref[n - 1] = s.astype(BF16)
        return carry

    lax.fori_loop(0, n_chunks - 1, bwd, 0)

    def out(n, carry):
        r = rows(n)
        q, k, v = ql_ref[r, :], kl_ref[r, :], vl_ref[r, :]
        p = (_dot_nt(q, k) * mask).astype(BF16)
        o = _dot(p, v)
        o = o + _dot(_scaled_bf16(q, xi_f), sf_ref[n])
        o = o + _dot(_scaled_bf16(q, xi_b), sb_ref[n])
        ol_ref[r, :] = _head_norm_gate(o, gl_ref[r, :])
        return carry

    lax.fori_loop(0, n_chunks, out, 0)


def _ret_call(batch, seq, ctx_len, proj, dec_tab):
    chunk = min(RET_CHUNK, seq)
    assert seq % chunk == 0
    n_chunks = seq // chunk
    ctx_row0 = batch * seq // ctx_len
    kq = RET_HEADS
    kv = 2 * RET_HEADS * RET_DK // RET_DV
    kg = kv + RET_HEADS
    vd = RET_HEADS * RET_DV
    in_specs = [
        pl.BlockSpec((seq, RET_DK), lambda b, h: (b, h)),
        pl.BlockSpec((seq, RET_DK), lambda b, h: (b, kq + h)),
        pl.BlockSpec((seq, RET_DV), lambda b, h: (b, kv + h)),
        pl.BlockSpec((seq, RET_DV), lambda b, h: (b, kg + h)),
        pl.BlockSpec((ctx_len, RET_DK), lambda b, h: (ctx_row0 + b, h)),
        pl.BlockSpec((ctx_len, RET_DK), lambda b, h: (ctx_row0 + b, kq + h)),
        pl.BlockSpec((ctx_len, RET_DV), lambda b, h: (ctx_row0 + b, kv + h)),
        pl.BlockSpec((ctx_len, RET_DV), lambda b, h: (ctx_row0 + b, kg + h)),
        pl.BlockSpec((1, 8, 128), lambda b, h: (h, 0, 0)),
    ]
    return pl.pallas_call(
        functools.partial(_ret_kernel, n_chunks=n_chunks, chunk=chunk, ctx_len=ctx_len),
        grid=(batch, RET_HEADS),
        in_specs=in_specs,
        out_specs=[pl.BlockSpec((seq, RET_DV), lambda b, h: (b, h)),
                   pl.BlockSpec((ctx_len, RET_DV), lambda b, h: (b, h))],
        out_shape=[jax.ShapeDtypeStruct((batch * seq, vd), BF16),
                   jax.ShapeDtypeStruct((batch * ctx_len, vd), BF16)],
        scratch_shapes=[pltpu.VMEM((n_chunks, RET_DK, RET_DV), BF16),
                        pltpu.VMEM((n_chunks, RET_DK, RET_DV), BF16),
                        pltpu.VMEM((RET_DK, RET_DV), F32)],
        compiler_params=_params(2),
        name="ret",
    )(proj, proj, proj, proj, proj, proj, proj, proj, dec_tab)


def _softmax_rows(s_ref, rs, slabs, exp2_scale, store):
    m = None
    for c0 in slabs:
        t = s_ref[rs, c0:c0 + LANES]
        m = t if m is None else jnp.maximum(m, t)
    m = jnp.max(m, axis=-1, keepdims=True)
    acc = None
    for c0 in slabs:
        e = jnp.exp2((s_ref[rs, c0:c0 + LANES] - m) * exp2_scale)
        acc = e if acc is None else acc + e
        store(c0, e.astype(BF16))
    return jnp.sum(acc, axis=-1, keepdims=True)


def _interleave(a, b):
    ia = ib = 0
    while ia < len(a) or ib < len(b):
        if ib >= len(b) or (ia < len(a) and ia * len(b) <= ib * len(a)):
            a[ia]()
            ia += 1
        else:
            b[ib]()
            ib += 1


def _att_kernel(q_ref, kl_ref, vl_ref, kc_ref, vc_ref, o_ref, s_ref, pa_ref, pb_ref, la_ref, lb_ref,
                *, n_q, seq):
    j = pl.program_id(2)
    group = ATT_HEADS // ATT_KV_HEADS
    tq = q_ref.shape[0]
    n_keys = s_ref.shape[1]
    exp2_scale = ATT_HD ** -0.5 * 1.4426950408889634
    slabs = range(0, n_keys, LANES)

    chunks = []
    for k0 in range(0, n_keys, ATT_KEY_CHUNK):
        chunks.append((k0, min(k0 + ATT_KEY_CHUNK, seq if k0 < seq else n_keys)))

    def key_rows(ref_lat, ref_ctx, k0, k1):
        return ref_lat[k0:k1, :] if k0 < seq else ref_ctx[k0 - seq:k1 - seq, :]

    def score_items(g):
        rs = slice(g * tq, (g + 1) * tq)

        def item(k0, k1):
            q = q_ref[:, g * ATT_HD:(g + 1) * ATT_HD]
            s_ref[rs, k0:k1] = _dot_nt(q, key_rows(kl_ref, kc_ref, k0, k1))

        return [functools.partial(item, k0, k1) for k0, k1 in chunks]

    def pv_items(g, p_ref, l_ref):
        rs = slice(g * tq, (g + 1) * tq)
        st = {"o": None}

        def item(k0, k1):
            part = _dot(p_ref[rs, k0:k1], key_rows(vl_ref, vc_ref, k0, k1))
            st["o"] = part if st["o"] is None else st["o"] + part

        def finish():
            o = jnp.where(pl.program_id(1) == 0, st["o"][:, :ATT_HD], st["o"][:, ATT_HD:])
            o_ref[:, g * ATT_HD:(g + 1) * ATT_HD] = (o / l_ref[rs, :]).astype(BF16)

        return [functools.partial(item, k0, k1) for k0, k1 in chunks] + [finish]

    def softmax_items(g, p_ref, l_ref):
        rs = slice(g * tq, (g + 1) * tq)
        st = {"m": None, "acc": None}

        def max_item(c0):
            t = s_ref[rs, c0:c0 + LANES]
            st["m"] = t if st["m"] is None else jnp.maximum(st["m"], t)

        def row_max():
            st["m"] = jnp.max(st["m"], axis=-1, keepdims=True)

        def exp_item(c0):
            e = jnp.exp2((s_ref[rs, c0:c0 + LANES] - st["m"]) * exp2_scale)
            st["acc"] = e if st["acc"] is None else st["acc"] + e
            p_ref[rs, c0:c0 + LANES] = e.astype(BF16)

        def finish():
            denom = jnp.sum(st["acc"], axis=-1, keepdims=True)
            l_ref[rs, :] = jnp.broadcast_to(denom, (tq, LANES))

        return ([functools.partial(max_item, c0) for c0 in slabs] + [row_max]
                + [functools.partial(exp_item, c0) for c0 in slabs] + [finish])

    def step(new, old):
        for stage in range(group + 1):
            mxu, vpu = [], []
            if new is not None and stage < group:
                mxu += score_items(stage)
            if old is not None and stage >= 1:
                mxu += pv_items(stage - 1, *old)
            if new is not None and stage >= 1:
                vpu += softmax_items(stage - 1, *new)
            _interleave(mxu, vpu)

    buf_a, buf_b = (pa_ref, la_ref), (pb_ref, lb_ref)
    even = (j % 2) == 0

    @pl.when(j == 0)
    def _():
        step(buf_a, None)

    @pl.when((j > 0) & (j < n_q) & even)
    def _():
        step(buf_a, buf_b)

    @pl.when((j < n_q) & jnp.logical_not(even))
    def _():
        step(buf_b, buf_a)

    @pl.when(j == n_q)
    def _():
        step(None, buf_b if n_q % 2 == 0 else buf_a)


def _att_call(batch, seq, ctx_len, proj):
    tq = ATT_Q_TILE
    assert seq % tq == 0 and ATT_KV_HEADS == 2
    n_q = seq // tq
    ctx_row0 = batch * seq // ctx_len
    group = ATT_HEADS // ATT_KV_HEADS
    group_w = group * ATT_HD
    k_col = ATT_HEADS
    v_col2 = (ATT_HEADS + ATT_KV_HEADS) // 2
    n_keys = seq + ctx_len
    return pl.pallas_call(
        functools.partial(_att_kernel, n_q=n_q, seq=seq),
        grid=(batch, ATT_KV_HEADS, n_q + 1),
        scratch_shapes=[pltpu.VMEM((group * tq, n_keys), F32),
                        pltpu.VMEM((group * tq, n_keys), BF16),
                        pltpu.VMEM((group * tq, n_keys), BF16),
                        pltpu.VMEM((group * tq, LANES), F32),
                        pltpu.VMEM((group * tq, LANES), F32)],
        in_specs=[
            pl.BlockSpec((tq, group_w), lambda b, kh, j: (b * n_q + jnp.minimum(j, n_q - 1), kh)),
            pl.BlockSpec((seq, ATT_HD), lambda b, kh, j: (b, k_col + kh)),
            pl.BlockSpec((seq, 2 * ATT_HD), lambda b, kh, j: (b, v_col2)),
            pl.BlockSpec((ctx_len, ATT_HD), lambda b, kh, j: (ctx_row0 + b, k_col + kh)),
            pl.BlockSpec((ctx_len, 2 * ATT_HD), lambda b, kh, j: (ctx_row0 + b, v_col2)),
        ],
        out_specs=pl.BlockSpec((tq, group_w),
                               lambda b, kh, j: (b * n_q + jnp.maximum(j - 1, 0), kh)),
        out_shape=jax.ShapeDtypeStruct((batch * seq, ATT_HEADS * ATT_HD), BF16),
        compiler_params=_params(3),
        name="att",
    )(proj, proj, proj, proj, proj)


def _att_ctx_kernel(q_ref, k_ref, v_ref, o_ref, s_ref):
    group = ATT_HEADS // ATT_KV_HEADS
    n = q_ref.shape[0]
    exp2_scale = ATT_HD ** -0.5 * 1.4426950408889634
    for h in range(ATT_HEADS):
        kv = slice((h // group) * ATT_HD, (h // group + 1) * ATT_HD)
        s_ref[...] = _dot_nt(q_ref[:, h * ATT_HD:(h + 1) * ATT_HD], k_ref[:, kv])
        es = {}

        def store(c0, e, es=es):
            es[c0] = e

        denom = _softmax_rows(s_ref, slice(0, n), range(0, n, LANES), exp2_scale, store)
        p = jnp.concatenate([es[c0] for c0 in sorted(es)], axis=1)
        o_ref[:, h * ATT_HD:(h + 1) * ATT_HD] = (_dot(p, v_ref[:, kv]) / denom).astype(BF16)


def _att_ctx_call(batch, seq, ctx_len, proj):
    ctx_row0 = batch * seq // ctx_len
    qd = ATT_HEADS * ATT_HD
    kd = ATT_KV_HEADS * ATT_HD
    assert qd % kd == 0
    return pl.pallas_call(
        _att_ctx_kernel,
        grid=(batch,),
        scratch_shapes=[pltpu.VMEM((ctx_len, ctx_len), F32)],
        in_specs=[
            pl.BlockSpec((ctx_len, qd), lambda b: (ctx_row0 + b, 0)),
            pl.BlockSpec((ctx_len, kd), lambda b: (ctx_row0 + b, qd // kd)),
            pl.BlockSpec((ctx_len, kd), lambda b: (ctx_row0 + b, qd // kd + 1)),
        ],
        out_specs=pl.BlockSpec((ctx_len, qd), lambda b: (b, 0)),
        out_shape=jax.ShapeDtypeStruct((batch * ctx_len, qd), BF16),
        compiler_params=_params(1),
        name="att_ctx",
    )(proj, proj, proj)


def _oproj_kernel(x_ref, mod_ref, w_ref, *rest, n_lat):
    gate = mod_ref[0, 0, 5:6, :]
    if len(rest) == 2:
        a_ref, o_ref = rest
        o_ref[...] = x_ref[...] + gate * _dot(a_ref[...], w_ref[...])
        return
    al_ref, ac_ref, o_ref = rest
    i = pl.program_id(0)

    @pl.when(i < n_lat)
    def _():
        o_ref[...] = x_ref[...] + gate * _dot(al_ref[...], w_ref[...])

    @pl.when(i >= n_lat)
    def _():
        o_ref[...] = x_ref[...] + gate * _dot(ac_ref[...], w_ref[...])


def _oproj_call(rows, n_tiles, x, mod, layer, w, which, a_lat, a_ctx=None):
    d = x.shape[1]
    k = w.shape[-2]
    n_lat = rows.n_lat
    in_specs = [rows.row_spec(d), rows.mod_spec(layer, d), _resident(w, (which,))]
    args = [x, mod, w]
    if a_ctx is None:
        in_specs.append(rows.row_spec(k))
        args.append(a_lat)
    else:
        in_specs.append(pl.BlockSpec((ROW_TILE, k), lambda i: (jnp.minimum(i, n_lat - 1), 0)))
        in_specs.append(pl.BlockSpec((ROW_TILE, k), lambda i: (jnp.maximum(i - n_lat, 0), 0)))
        args += [a_lat, a_ctx]
    return pl.pallas_call(
        functools.partial(_oproj_kernel, n_lat=n_lat),
        grid=(n_tiles,),
        in_specs=in_specs,
        out_specs=rows.row_spec(d),
        out_shape=jax.ShapeDtypeStruct((n_tiles * ROW_TILE, d), F32),
        compiler_params=_params(1),
        name="oproj",
    )(*args)


def _with_identity_rows(cos, sin):
    pad = (ROW_TILE, cos.shape[1])
    return (jnp.concatenate([cos, jnp.ones(pad, F32)], axis=0),
            jnp.concatenate([sin, jnp.zeros(pad, F32)], axis=0))


def _seq_rope_tables(seq):
    half = RET_DK // 2
    freqs = ROPE_THETA ** (-jnp.arange(half, dtype=F32) / half)
    ang = jnp.arange(seq, dtype=F32)[:, None] * freqs
    return _with_identity_rows(jnp.cos(ang), jnp.sin(ang))


def _axial_rope_tables(seq):
    quarter = ATT_HD // 4
    tok = jnp.arange(seq)
    r = (tok // GRID_W).astype(F32)
    cl = (tok % GRID_W).astype(F32)
    freqs = ROPE_THETA ** (-jnp.arange(quarter, dtype=F32) / quarter)
    ang = jnp.concatenate([r[:, None] * freqs, cl[:, None] * freqs], axis=-1)
    cos, sin = jnp.cos(ang), jnp.sin(ang)
    return _with_identity_rows(jnp.concatenate([cos, cos], axis=-1),
                               jnp.concatenate([-sin, sin], axis=-1))


def kernel(x, c, ctx, c_ctx, ada_w, ada_b, norm_g, ffn_w1, ffn_w2, ret_w_in, ret_w_out,
           ret_decay_f, ret_decay_b, att_w_qkv, att_w_o, att_q_gain, att_k_gain, final_g):
    batch, seq, d = x.shape
    ctx_len = ctx.shape[1]
    depth = ada_w.shape[0]
    rows = _Rows(batch, seq, ctx_len)

    cc = jnp.concatenate(
        [c, c_ctx[None], jnp.zeros((ADA_ROWS - batch - 1, d), F32)], axis=0)
    mod = _ada_call(cc, ada_w, ada_b).reshape(depth, ADA_ROWS, N_MOD, d)

    w1 = ffn_w1.astype(BF16)
    w2 = ffn_w2.astype(BF16)
    w_in = ret_w_in.astype(BF16)
    w_out = ret_w_out.astype(BF16)
    w_qkv = att_w_qkv.astype(BF16)
    w_o = att_w_o.astype(BF16)

    ret_cos, ret_sin = _seq_rope_tables(seq)
    att_cos, att_sin = _axial_rope_tables(seq)
    ret_rope_specs = [rows.rope_spec(RET_DK // 2)] * 2
    att_rope_specs = [rows.rope_spec(ATT_HD)] * 2
    gain_spec = pl.BlockSpec((1, ATT_HD), lambda i: (0, 0))

    xs = x.reshape(batch * seq, d)
    xs_ctx = ctx.reshape(batch * ctx_len, d)

    for i in range(depth):
        last = i == depth - 1
        j = i // 2
        xs = _ffn_call(rows, rows.n_all, xs, mod, i, 0, norm_g[i, 0], w1, w2, 0,
                       x_ctx=xs_ctx if i == 0 else None)
        n_tiles = rows.n_lat if last else rows.n_all
        if i % 2 == 0:
            proj = _proj_call(_proj_ret_kernel, "proj_ret", rows, xs, mod, i, norm_g[i, 1],
                              w_in, j, [ret_cos, ret_sin], ret_rope_specs)
            dec_tab = jnp.broadcast_to(
                jnp.stack([ret_decay_f[j], ret_decay_b[j]], axis=1)[:, :, None],
                (RET_HEADS, 2, 128))
            dec_tab = jnp.concatenate([dec_tab, jnp.zeros((RET_HEADS, 6, 128), F32)], axis=1)
            a_lat, a_ctx = _ret_call(batch, seq, ctx_len, proj, dec_tab)
            xs = _oproj_call(rows, n_tiles, xs, mod, i, w_out, j, a_lat,
                             None if last else a_ctx)
        else:
            proj = _proj_call(_proj_att_kernel, "proj_att", rows, xs, mod, i, norm_g[i, 1],
                              w_qkv, j,
                              [att_q_gain[j].reshape(1, ATT_HD), att_k_gain[j].reshape(1, ATT_HD),
                               att_cos, att_sin],
                              [gain_spec, gain_spec] + att_rope_specs)
            a_lat = _att_call(batch, seq, ctx_len, proj)
            a_ctx = None if last else _att_ctx_call(batch, seq, ctx_len, proj)
            xs = _oproj_call(rows, n_tiles, xs, mod, i, w_o, j, a_lat, a_ctx)
        xs = _ffn_call(rows, n_tiles, xs, mod, i, 6, norm_g[i, 2], w1, w2, 1,
                       final_g if last else None)
    return xs.reshape(batch, seq, d)
```

```python
import functools

import jax
import jax.numpy as jnp
from jax import lax
from jax.experimental import pallas as pl
from jax.experimental.pallas import tpu as pltpu

F32 = jnp.float32
BF16 = jnp.bfloat16

N_MOD = 9
RET_HEADS = 4
RET_DK = 256
RET_DV = 512
ATT_HEADS = 8
ATT_KV_HEADS = 2
ATT_HD = 128
GRID_W = 64
ROPE_THETA = 10000.0
EPS = 1e-6

ROW_TILE = 512
RET_CHUNK = 512
ATT_Q_TILE = 256
ATT_KEY_CHUNK = 512
LANES = 128
MXU_DEPTH = 256
FFN_SPLIT = 2
ADA_ROWS = 16
VMEM_LIMIT = 56 * 1024 * 1024


def _params(n_axes):
    return pltpu.CompilerParams(
        dimension_semantics=("arbitrary",) * n_axes, vmem_limit_bytes=VMEM_LIMIT)


def _resident(stacked, lead):
    tail = stacked.shape[len(lead):]
    index = tuple(lead) + (0,) * len(tail)
    return pl.BlockSpec((None,) * len(lead) + tail, lambda *_: index,
                        pipeline_mode=pl.Buffered(1))


def _silu(x):
    return x * jax.nn.sigmoid(x)


def _rms_rows(x):
    return x * lax.rsqrt(jnp.mean(x * x, axis=-1, keepdims=True) + EPS)


def _modulated(x, gain, shift, scale):
    return (_rms_rows(x) * gain) * (1.0 + scale) + shift


def _dot(a, b):
    return jnp.dot(a, b, preferred_element_type=F32)


def _dot_nt(a, b):
    return lax.dot_general(a, b, (((1,), (1,)), ((), ())), preferred_element_type=F32)


def _dot_tn(a, b):
    return lax.dot_general(a, b, (((0,), (0,)), ((), ())), preferred_element_type=F32)


def _ada_kernel(cc_ref, w_ref, b_ref, o_ref):
    s = _silu(cc_ref[...])
    o_ref[0] = _dot(s.astype(BF16), w_ref[0].astype(BF16)) + b_ref[0]


def _ada_call(cc, ada_w, ada_b):
    depth, d, n = ada_w.shape
    tn = 1024
    return pl.pallas_call(
        _ada_kernel,
        grid=(depth, n // tn),
        in_specs=[
            pl.BlockSpec((ADA_ROWS, d), lambda l, j: (0, 0)),
            pl.BlockSpec((1, d, tn), lambda l, j: (l, 0, j)),
            pl.BlockSpec((1, 1, tn), lambda l, j: (l, 0, j)),
        ],
        out_specs=pl.BlockSpec((1, ADA_ROWS, tn), lambda l, j: (l, 0, j)),
        out_shape=jax.ShapeDtypeStruct((depth, ADA_ROWS, n), F32),
        compiler_params=_params(2),
        name="ada",
    )(cc, ada_w, ada_b.reshape(depth, 1, n))


class _Rows:
    def __init__(self, batch, seq, ctx_len):
        self.batch, self.seq, self.ctx_len = batch, seq, ctx_len
        assert seq % ROW_TILE == 0 and (batch * ctx_len) % ROW_TILE == 0
        self.tiles_per_batch = seq // ROW_TILE
        self.n_lat = batch * seq // ROW_TILE
        self.n_ctx = batch * ctx_len // ROW_TILE
        self.n_all = self.n_lat + self.n_ctx

    def mod_spec(self, layer, d):
        n_lat, tpb, batch = self.n_lat, self.tiles_per_batch, self.batch
        return pl.BlockSpec(
            (1, 1, N_MOD, d),
            lambda i: (layer, jnp.where(i < n_lat, i // tpb, batch), 0, 0))

    def rope_spec(self, width):
        n_lat, tpb = self.n_lat, self.tiles_per_batch
        return pl.BlockSpec((ROW_TILE, width), lambda i: (jnp.where(i < n_lat, i % tpb, tpb), 0))

    @staticmethod
    def row_spec(width):
        return pl.BlockSpec((ROW_TILE, width), lambda i: (i, 0))


def _mod_rows(mod_ref, base):
    return [mod_ref[0, 0, base + k:base + k + 1, :] for k in range(3)]


def _ffn_kernel(*refs, base, ffn_dim, chunks, final, n_lat_split):
    refs = list(refs)
    o_ref = refs.pop()
    fg_ref = refs.pop() if final else None
    if n_lat_split is None:
        x_ref, mod_ref, g_ref, w1_ref, w2_ref = refs
        x = x_ref[...]
    else:
        x_ref, xc_ref, mod_ref, g_ref, w1_ref, w2_ref = refs
        x = jnp.where(pl.program_id(0) < n_lat_split, x_ref[...], xc_ref[...])
    shift, scale, gate = _mod_rows(mod_ref, base)
    h = _modulated(x, g_ref[...], shift, scale).astype(BF16)
    acc = None
    lo = 0
    for width in chunks:
        gt = _dot(h, w1_ref[:, lo:lo + width])
        up = _dot(h, w1_ref[:, ffn_dim + lo:ffn_dim + lo + width])
        a = (_silu(gt) * up).astype(BF16)
        part = _dot(a, w2_ref[lo:lo + width, :])
        acc = part if acc is None else acc + part
        lo += width
    y = x + (0.5 * gate) * acc
    if final:
        y = _rms_rows(y) * fg_ref[...]
    o_ref[...] = y


def _ffn_chunks(ffn_dim):
    assert ffn_dim % MXU_DEPTH == 0
    n_tiles = ffn_dim // MXU_DEPTH
    first = (n_tiles // FFN_SPLIT) * MXU_DEPTH
    sizes = [first] * (FFN_SPLIT - 1) + [ffn_dim - first * (FFN_SPLIT - 1)]
    return tuple(s for s in sizes if s)


def _ffn_call(rows, n_tiles, x, mod, layer, base, gain, w1, w2, which, final_g=None, x_ctx=None):
    d = x.shape[1]
    ffn_dim = w2.shape[-2]
    chunks = _ffn_chunks(ffn_dim)
    final = final_g is not None
    n_lat = rows.n_lat
    if x_ctx is None:
        in_specs, args = [rows.row_spec(d)], [x]
    else:
        in_specs = [pl.BlockSpec((ROW_TILE, d), lambda i: (jnp.minimum(i, n_lat - 1), 0)),
                    pl.BlockSpec((ROW_TILE, d), lambda i: (jnp.maximum(i - n_lat, 0), 0))]
        args = [x, x_ctx]
    in_specs += [rows.mod_spec(layer, d), pl.BlockSpec((1, d), lambda i: (0, 0)),
                 _resident(w1, (layer, which)), _resident(w2, (layer, which))]
    args += [mod, gain.reshape(1, d), w1, w2]
    if final:
        in_specs.append(pl.BlockSpec((1, d), lambda i: (0, 0)))
        args.append(final_g.reshape(1, d))
    return pl.pallas_call(
        functools.partial(_ffn_kernel, base=base, ffn_dim=ffn_dim, chunks=chunks, final=final,
                          n_lat_split=None if x_ctx is None else n_lat),
        grid=(n_tiles,),
        in_specs=in_specs,
        out_specs=rows.row_spec(d),
        out_shape=jax.ShapeDtypeStruct((n_tiles * ROW_TILE, d), F32),
        compiler_params=_params(1),
        name="ffn",
    )(*args)


def _rope_halves(x1, x2, cos, sin):
    return x1 * cos - x2 * sin, x2 * cos + x1 * sin


def _proj_ret_kernel(x_ref, mod_ref, g_ref, w_ref, cos_ref, sin_ref, o_ref):
    shift, scale, _ = _mod_rows(mod_ref, 3)
    h = _modulated(x_ref[...], g_ref[...], shift, scale).astype(BF16)
    cos, sin = cos_ref[...], sin_ref[...]
    qd = RET_HEADS * RET_DK
    vd = RET_HEADS * RET_DV
    half = RET_DK // 2
    for part, mult in ((0, None), (1, RET_DK ** -0.5)):
        p = _dot(h, w_ref[:, part * qd:(part + 1) * qd])
        if mult is not None:
            p = p * mult
        for hh in range(RET_HEADS):
            lo = hh * RET_DK
            r1, r2 = _rope_halves(p[:, lo:lo + half], p[:, lo + half:lo + RET_DK], cos, sin)
            o_ref[:, part * qd + lo:part * qd + lo + half] = r1.astype(BF16)
            o_ref[:, part * qd + lo + half:part * qd + lo + RET_DK] = r2.astype(BF16)
    for part in range(2):
        lo = 2 * qd + part * vd
        o_ref[:, lo:lo + vd] = _dot(h, w_ref[:, lo:lo + vd]).astype(BF16)


def _proj_att_kernel(x_ref, mod_ref, g_ref, w_ref, qg_ref, kg_ref, cos_ref, sin_ref, o_ref):
    shift, scale, _ = _mod_rows(mod_ref, 3)
    h = _modulated(x_ref[...], g_ref[...], shift, scale).astype(BF16)
    cos, sin = cos_ref[...], sin_ref[...]
    qd = ATT_HEADS * ATT_HD
    kd = ATT_KV_HEADS * ATT_HD

    def norm_rope(p, n_heads, gain, out_lo):
        for hh in range(n_heads):
            lo = hh * ATT_HD
            y = _rms_rows(p[:, lo:lo + ATT_HD]) * gain
            r = y * cos + pltpu.roll(y, ATT_HD // 2, 1) * sin
            o_ref[:, out_lo + lo:out_lo + lo + ATT_HD] = r.astype(BF16)

    norm_rope(_dot(h, w_ref[:, :qd]), ATT_HEADS, qg_ref[...], 0)
    norm_rope(_dot(h, w_ref[:, qd:qd + kd]), ATT_KV_HEADS, kg_ref[...], qd)
    o_ref[:, qd + kd:] = _dot(h, w_ref[:, qd + kd:]).astype(BF16)


def _proj_call(kernel, name, rows, x, mod, layer, gain, w, which, extras, extra_specs):
    d = x.shape[1]
    n_out = w.shape[-1]
    return pl.pallas_call(
        kernel,
        grid=(rows.n_all,),
        in_specs=[rows.row_spec(d), rows.mod_spec(layer, d), pl.BlockSpec((1, d), lambda i: (0, 0)),
                  _resident(w, (which,))] + extra_specs,
        out_specs=rows.row_spec(n_out),
        out_shape=jax.ShapeDtypeStruct((rows.n_all * ROW_TILE, n_out), BF16),
        compiler_params=_params(1),
        name=name,
    )(x, mod, gain.reshape(1, d), w, *extras)


def _decay_tables(n, lg_f, lg_b):
    i = lax.broadcasted_iota(jnp.int32, (n, n), 0)
    j = lax.broadcasted_iota(jnp.int32, (n, n), 1)
    diff = (i - j).astype(F32)
    mask = (jnp.where(diff >= 0, jnp.exp(jnp.maximum(diff, 0.0) * lg_f), 0.0)
            + jnp.where(diff <= 0, jnp.exp(jnp.maximum(-diff, 0.0) * lg_b), 0.0))
    pos = lax.broadcasted_iota(jnp.int32, (n, 1), 0).astype(F32)
    xi_f = jnp.exp((pos + 1.0) * lg_f)
    xi_b = jnp.exp((n - pos) * lg_b)
    zeta_f = jnp.exp((n - 1.0 - pos) * lg_f)
    zeta_b = jnp.exp(pos * lg_b)
    return mask, xi_f, xi_b, zeta_f, zeta_b


def _scaled_bf16(x, col):
    return (x.astype(F32) * col).astype(BF16)


def _head_norm_gate(o, g):
    return (_rms_rows(o) * _silu(g.astype(F32))).astype(BF16)


def _scaled_pair(x, col_a, col_b):
    xf = x.astype(F32)
    return jnp.concatenate([(xf * col_a).astype(BF16), (xf * col_b).astype(BF16)], axis=1)


def _ret_kernel(ql_ref, kl_ref, vl_ref, gl_ref, qc_ref, kc_ref, vc_ref, gc_ref, dec_ref,
                ol_ref, oc_ref, st_ref, ub_ref, runf_ref, runb_ref, mask_ref, cmask_ref,
                col_ref, ccol_ref, *, n_chunks, chunk, ctx_len):
    dk = RET_DK
    lg_f = -jnp.exp(dec_ref[0, 0:1, 0:1])
    lg_b = -jnp.exp(dec_ref[0, 1:2, 0:1])

    @pl.when(pl.program_id(1) == 0)
    def _():
        mask, xi_f, xi_b, zeta_f, zeta_b = _decay_tables(chunk, lg_f, lg_b)
        mask_ref[...] = mask
        for k, col in enumerate((xi_f, xi_b, zeta_f, zeta_b)):
            col_ref[k] = col
        cmask, _, _, czeta_f, czeta_b = _decay_tables(ctx_len, lg_f, lg_b)
        cmask_ref[...] = cmask
        ccol_ref[0] = czeta_f
        ccol_ref[1] = czeta_b

    decay_f = jnp.exp(chunk * lg_f)
    decay_b = jnp.exp(chunk * lg_b)

    q, k, v = qc_ref[...], kc_ref[...], vc_ref[...]
    p = (_dot_nt(q, k) * cmask_ref[...]).astype(BF16)
    oc_ref[...] = _head_norm_gate(_dot(p, v), gc_ref[...])
    s0 = _dot_tn(_scaled_pair(k, ccol_ref[0], ccol_ref[1]), v)
    runf_ref[...] = s0[:dk]
    runb_ref[...] = s0[dk:]
    st_ref[0, :dk, :] = s0[:dk].astype(BF16)
    st_ref[n_chunks - 1, dk:, :] = s0[dk:].astype(BF16)

    def rows(n):
        return slice(n * chunk, (n + 1) * chunk)

    for n in range(n_chunks):
        u = _dot_tn(_scaled_pair(kl_ref[rows(n), :], col_ref[2], col_ref[3]), vl_ref[rows(n), :])
        if n + 1 < n_chunks:
            s = runf_ref[...] * decay_f + u[:dk]
            runf_ref[...] = s
            st_ref[n + 1, :dk, :] = s.astype(BF16)
        if n > 0:
            ub_ref[n] = u[dk:]

    for n in range(n_chunks - 1, 0, -1):
        s = runb_ref[...] * decay_b + ub_ref[n]
        runb_ref[...] = s
        st_ref[n - 1, dk:, :] = s.astype(BF16)

    for n in range(n_chunks):
        r = rows(n)
        q, k, v = ql_ref[r, :], kl_ref[r, :], vl_ref[r, :]
        p = (_dot_nt(q, k) * mask_ref[...]).astype(BF16)
        o = _dot(p, v) + _dot(_scaled_pair(q, col_ref[0], col_ref[1]), st_ref[n])
        ol_ref[r, :] = _head_norm_gate(o, gl_ref[r, :])


def _ret_call(batch, seq, ctx_len, proj, dec_tab):
    chunk = min(RET_CHUNK, seq)
    assert seq % chunk == 0
    n_chunks = seq // chunk
    ctx_row0 = batch * seq // ctx_len
    kq = RET_HEADS
    kv = 2 * RET_HEADS * RET_DK // RET_DV
    kg = kv + RET_HEADS
    vd = RET_HEADS * RET_DV
    in_specs = [
        pl.BlockSpec((seq, RET_DK), lambda h, b: (b, h)),
        pl.BlockSpec((seq, RET_DK), lambda h, b: (b, kq + h)),
        pl.BlockSpec((seq, RET_DV), lambda h, b: (b, kv + h)),
        pl.BlockSpec((seq, RET_DV), lambda h, b: (b, kg + h)),
        pl.BlockSpec((ctx_len, RET_DK), lambda h, b: (ctx_row0 + b, h)),
        pl.BlockSpec((ctx_len, RET_DK), lambda h, b: (ctx_row0 + b, kq + h)),
        pl.BlockSpec((ctx_len, RET_DV), lambda h, b: (ctx_row0 + b, kv + h)),
        pl.BlockSpec((ctx_len, RET_DV), lambda h, b: (ctx_row0 + b, kg + h)),
        pl.BlockSpec((1, 8, 128), lambda h, b: (h, 0, 0)),
    ]
    return pl.pallas_call(
        functools.partial(_ret_kernel, n_chunks=n_chunks, chunk=chunk, ctx_len=ctx_len),
        grid=(RET_HEADS, batch),
        in_specs=in_specs,
        out_specs=[pl.BlockSpec((seq, RET_DV), lambda h, b: (b, h)),
                   pl.BlockSpec((ctx_len, RET_DV), lambda h, b: (b, h))],
        out_shape=[jax.ShapeDtypeStruct((batch * seq, vd), BF16),
                   jax.ShapeDtypeStruct((batch * ctx_len, vd), BF16)],
        scratch_shapes=[pltpu.VMEM((n_chunks, 2 * RET_DK, RET_DV), BF16),
                        pltpu.VMEM((n_chunks, RET_DK, RET_DV), F32),
                        pltpu.VMEM((RET_DK, RET_DV), F32),
                        pltpu.VMEM((RET_DK, RET_DV), F32),
                        pltpu.VMEM((chunk, chunk), F32),
                        pltpu.VMEM((ctx_len, ctx_len), F32),
                        pltpu.VMEM((4, chunk, 1), F32),
                        pltpu.VMEM((2, ctx_len, 1), F32)],
        compiler_params=_params(2),
        name="ret",
    )(proj, proj, proj, proj, proj, proj, proj, proj, dec_tab)


def _att_kernel(q_ref, kl_ref, vl_ref, kc_ref, vc_ref, o_ref, s_ref, p_ref, *, n_q, seq):
    qi = pl.program_id(2)
    group = ATT_HEADS // ATT_KV_HEADS
    tq = q_ref.shape[0]
    n_keys = s_ref.shape[1]
    exp2_scale = ATT_HD ** -0.5 * 1.4426950408889634

    def attend(col_lo):
        slabs = range(col_lo, n_keys, LANES)
        for g in range(group):
            q = q_ref[:, g * ATT_HD:(g + 1) * ATT_HD]
            if col_lo < seq:
                s_ref[g * tq:(g + 1) * tq, :seq] = _dot_nt(q, kl_ref[...])
            s_ref[g * tq:(g + 1) * tq, seq:] = _dot_nt(q, kc_ref[...])
        for g in range(group):
            rs = slice(g * tq, (g + 1) * tq)
            m = None
            for c0 in slabs:
                t = s_ref[rs, c0:c0 + LANES]
                m = t if m is None else jnp.maximum(m, t)
            m = jnp.max(m, axis=-1, keepdims=True)
            acc = None
            o = None
            for k0 in range(col_lo, n_keys, ATT_KEY_CHUNK):
                k1 = min(k0 + ATT_KEY_CHUNK, seq if k0 < seq else n_keys)
                for c0 in range(k0, k1, LANES):
                    e = jnp.exp2((s_ref[rs, c0:c0 + LANES] - m) * exp2_scale)
                    acc = e if acc is None else acc + e
                    p_ref[rs, c0:c0 + LANES] = e.astype(BF16)
                v = vl_ref[k0:k1, :] if k0 < seq else vc_ref[k0 - seq:k1 - seq, :]
                part = _dot(p_ref[rs, k0:k1], v)
                o = part if o is None else o + part
            o = jnp.where(pl.program_id(1) == 0, o[:, :ATT_HD], o[:, ATT_HD:])
            o = o / jnp.sum(acc, axis=-1, keepdims=True)
            o_ref[:, g * ATT_HD:(g + 1) * ATT_HD] = o.astype(BF16)

    @pl.when(qi < n_q)
    def _():
        attend(0)

    @pl.when(qi >= n_q)
    def _():
        attend(seq)


def _att_call(batch, seq, ctx_len, proj, ctx_out):
    tq = ATT_Q_TILE
    assert ctx_len == tq and seq % tq == 0
    n_q = seq // tq
    ctx_row0 = batch * seq // ctx_len
    group_w = (ATT_HEADS // ATT_KV_HEADS) * ATT_HD
    k_col = ATT_HEADS
    v_col = ATT_HEADS + ATT_KV_HEADS
    n_rows = batch * seq + (batch * ctx_len if ctx_out else 0)

    def q_map(b, kh, qi):
        return (jnp.where(qi < n_q, b * n_q + qi, ctx_row0 + b), kh)

    return pl.pallas_call(
        functools.partial(_att_kernel, n_q=n_q, seq=seq),
        grid=(batch, ATT_KV_HEADS, n_q + (1 if ctx_out else 0)),
        scratch_shapes=[pltpu.VMEM((group_w // ATT_HD * tq, seq + ctx_len), F32),
                        pltpu.VMEM((group_w // ATT_HD * tq, seq + ctx_len), BF16)],
        in_specs=[
            pl.BlockSpec((tq, group_w), q_map),
            pl.BlockSpec((seq, ATT_HD), lambda b, kh, qi: (b, k_col + kh)),
            pl.BlockSpec((seq, 2 * ATT_HD), lambda b, kh, qi: (b, v_col // 2)),
            pl.BlockSpec((ctx_len, ATT_HD), lambda b, kh, qi: (ctx_row0 + b, k_col + kh)),
            pl.BlockSpec((ctx_len, 2 * ATT_HD), lambda b, kh, qi: (ctx_row0 + b, v_col // 2)),
        ],
        out_specs=pl.BlockSpec((tq, group_w), q_map),
        out_shape=jax.ShapeDtypeStruct((n_rows, ATT_HEADS * ATT_HD), BF16),
        compiler_params=_params(3),
        name="att",
    )(proj, proj, proj, proj, proj)


def _oproj_kernel(x_ref, mod_ref, w_ref, *rest, n_lat):
    gate = mod_ref[0, 0, 5:6, :]
    if len(rest) == 2:
        a_ref, o_ref = rest
        o_ref[...] = x_ref[...] + gate * _dot(a_ref[...], w_ref[...])
        return
    al_ref, ac_ref, o_ref = rest
    i = pl.program_id(0)

    @pl.when(i < n_lat)
    def _():
        o_ref[...] = x_ref[...] + gate * _dot(al_ref[...], w_ref[...])

    @pl.when(i >= n_lat)
    def _():
        o_ref[...] = x_ref[...] + gate * _dot(ac_ref[...], w_ref[...])


def _oproj_call(rows, n_tiles, x, mod, layer, w, which, a_lat, a_ctx=None):
    d = x.shape[1]
    k = w.shape[-2]
    n_lat = rows.n_lat
    in_specs = [rows.row_spec(d), rows.mod_spec(layer, d), _resident(w, (which,))]
    args = [x, mod, w]
    if a_ctx is None:
        in_specs.append(rows.row_spec(k))
        args.append(a_lat)
    else:
        in_specs.append(pl.BlockSpec((ROW_TILE, k), lambda i: (jnp.minimum(i, n_lat - 1), 0)))
        in_specs.append(pl.BlockSpec((ROW_TILE, k), lambda i: (jnp.maximum(i - n_lat, 0), 0)))
        args += [a_lat, a_ctx]
    return pl.pallas_call(
        functools.partial(_oproj_kernel, n_lat=n_lat),
        grid=(n_tiles,),
        in_specs=in_specs,
        out_specs=rows.row_spec(d),
        out_shape=jax.ShapeDtypeStruct((n_tiles * ROW_TILE, d), F32),
        compiler_params=_params(1),
        name="oproj",
    )(*args)


def _with_identity_rows(cos, sin):
    pad = (ROW_TILE, cos.shape[1])
    return (jnp.concatenate([cos, jnp.ones(pad, F32)], axis=0),
            jnp.concatenate([sin, jnp.zeros(pad, F32)], axis=0))


def _seq_rope_tables(seq):
    half = RET_DK // 2
    freqs = ROPE_THETA ** (-jnp.arange(half, dtype=F32) / half)
    ang = jnp.arange(seq, dtype=F32)[:, None] * freqs
    return _with_identity_rows(jnp.cos(ang), jnp.sin(ang))


def _axial_rope_tables(seq):
    quarter = ATT_HD // 4
    tok = jnp.arange(seq)
    r = (tok // GRID_W).astype(F32)
    cl = (tok % GRID_W).astype(F32)
    freqs = ROPE_THETA ** (-jnp.arange(quarter, dtype=F32) / quarter)
    ang = jnp.concatenate([r[:, None] * freqs, cl[:, None] * freqs], axis=-1)
    cos, sin = jnp.cos(ang), jnp.sin(ang)
    return _with_identity_rows(jnp.concatenate([cos, cos], axis=-1),
                               jnp.concatenate([-sin, sin], axis=-1))


def kernel(x, c, ctx, c_ctx, ada_w, ada_b, norm_g, ffn_w1, ffn_w2, ret_w_in, ret_w_out,
           ret_decay_f, ret_decay_b, att_w_qkv, att_w_o, att_q_gain, att_k_gain, final_g):
    batch, seq, d = x.shape
    ctx_len = ctx.shape[1]
    depth = ada_w.shape[0]
    rows = _Rows(batch, seq, ctx_len)

    cc = jnp.concatenate(
        [c, c_ctx[None], jnp.zeros((ADA_ROWS - batch - 1, d), F32)], axis=0)
    mod = _ada_call(cc, ada_w, ada_b).reshape(depth, ADA_ROWS, N_MOD, d)

    w1 = ffn_w1.astype(BF16)
    w2 = ffn_w2.astype(BF16)
    w_in = ret_w_in.astype(BF16)
    w_out = ret_w_out.astype(BF16)
    w_qkv = att_w_qkv.astype(BF16)
    w_o = att_w_o.astype(BF16)

    ret_cos, ret_sin = _seq_rope_tables(seq)
    att_cos, att_sin = _axial_rope_tables(seq)
    ret_rope_specs = [rows.rope_spec(RET_DK // 2)] * 2
    att_rope_specs = [rows.rope_spec(ATT_HD)] * 2
    gain_spec = pl.BlockSpec((1, ATT_HD), lambda i: (0, 0))

    xs = x.reshape(batch * seq, d)
    xs_ctx = ctx.reshape(batch * ctx_len, d)

    for i in range(depth):
        last = i == depth - 1
        j = i // 2
        xs = _ffn_call(rows, rows.n_all, xs, mod, i, 0, norm_g[i, 0], w1, w2, 0,
                       x_ctx=xs_ctx if i == 0 else None)
        n_tiles = rows.n_lat if last else rows.n_all
        if i % 2 == 0:
            proj = _proj_call(_proj_ret_kernel, "proj_ret", rows, xs, mod, i, norm_g[i, 1],
                              w_in, j, [ret_cos, ret_sin], ret_rope_specs)
            dec_tab = jnp.broadcast_to(
                jnp.stack([ret_decay_f[j], ret_decay_b[j]], axis=1)[:, :, None],
                (RET_HEADS, 2, 128))
            dec_tab = jnp.concatenate([dec_tab, jnp.zeros((RET_HEADS, 6, 128), F32)], axis=1)
            a_lat, a_ctx = _ret_call(batch, seq, ctx_len, proj, dec_tab)
            xs = _oproj_call(rows, n_tiles, xs, mod, i, w_out, j, a_lat,
                             None if last else a_ctx)
        else:
            proj = _proj_call(_proj_att_kernel, "proj_att", rows, xs, mod, i, norm_g[i, 1],
                              w_qkv, j,
                              [att_q_gain[j].reshape(1, ATT_HD), att_k_gain[j].reshape(1, ATT_HD),
                               att_cos, att_sin],
                              [gain_spec, gain_spec] + att_rope_specs)
            a = _att_call(batch, seq, ctx_len, proj, not last)
            xs = _oproj_call(rows, n_tiles, xs, mod, i, w_o, j, a)
        xs = _ffn_call(rows, n_tiles, xs, mod, i, 6, norm_g[i, 2], w1, w2, 1,
                       final_g if last else None)
    return xs.reshape(batch, seq, d)
```

```python
import functools

import jax
import jax.numpy as jnp
from jax import lax
from jax.experimental import pallas as pl
from jax.experimental.pallas import tpu as pltpu

F32 = jnp.float32
BF16 = jnp.bfloat16

N_MOD = 9
RET_HEADS = 4
RET_DK = 256
RET_DV = 512
ATT_HEADS = 8
ATT_KV_HEADS = 2
ATT_HD = 128
GRID_W = 64
ROPE_THETA = 10000.0
EPS = 1e-6

ROW_TILE = 512
RET_CHUNK = 512
ATT_Q_TILE = 256
ATT_KEY_CHUNK = 512
LANES = 128
MXU_DEPTH = 256
FFN_SPLIT = 2
FFN_SUBTILES = 2
PROJ_ATT_SUBTILES = 2
ADA_ROWS = 16
VMEM_LIMIT = 56 * 1024 * 1024


def _params(n_axes):
    return pltpu.CompilerParams(
        dimension_semantics=("arbitrary",) * n_axes, vmem_limit_bytes=VMEM_LIMIT)


def _resident(stacked, lead):
    tail = stacked.shape[len(lead):]
    index = tuple(lead) + (0,) * len(tail)
    return pl.BlockSpec((None,) * len(lead) + tail, lambda *_: index,
                        pipeline_mode=pl.Buffered(1))


def _silu(x):
    return x * jax.nn.sigmoid(x)


def _rms_rows(x):
    return x * lax.rsqrt(jnp.mean(x * x, axis=-1, keepdims=True) + EPS)


def _modulated(x, gain, shift, scale):
    return (_rms_rows(x) * gain) * (1.0 + scale) + shift


def _dot(a, b):
    return jnp.dot(a, b, preferred_element_type=F32)


def _dot_nt(a, b):
    return lax.dot_general(a, b, (((1,), (1,)), ((), ())), preferred_element_type=F32)


def _dot_tn(a, b):
    return lax.dot_general(a, b, (((0,), (0,)), ((), ())), preferred_element_type=F32)


def _ada_kernel(cc_ref, w_ref, b_ref, o_ref):
    s = _silu(cc_ref[...])
    o_ref[0] = _dot(s.astype(BF16), w_ref[0].astype(BF16)) + b_ref[0]


def _ada_call(cc, ada_w, ada_b):
    depth, d, n = ada_w.shape
    tn = 1024
    return pl.pallas_call(
        _ada_kernel,
        grid=(depth, n // tn),
        in_specs=[
            pl.BlockSpec((ADA_ROWS, d), lambda l, j: (0, 0)),
            pl.BlockSpec((1, d, tn), lambda l, j: (l, 0, j)),
            pl.BlockSpec((1, 1, tn), lambda l, j: (l, 0, j)),
        ],
        out_specs=pl.BlockSpec((1, ADA_ROWS, tn), lambda l, j: (l, 0, j)),
        out_shape=jax.ShapeDtypeStruct((depth, ADA_ROWS, n), F32),
        compiler_params=_params(2),
        name="ada",
    )(cc, ada_w, ada_b.reshape(depth, 1, n))


class _Rows:
    def __init__(self, batch, seq, ctx_len, tile=ROW_TILE):
        self.batch, self.seq, self.ctx_len, self.tile = batch, seq, ctx_len, tile
        assert seq % tile == 0 and (batch * ctx_len) % tile == 0
        self.tiles_per_batch = seq // tile
        self.n_lat = batch * seq // tile
        self.n_ctx = batch * ctx_len // tile
        self.n_all = self.n_lat + self.n_ctx

    @classmethod
    def widest(cls, batch, seq, ctx_len, factor):
        tile = ROW_TILE * factor
        if seq % tile or (batch * ctx_len) % tile:
            tile = ROW_TILE
        return cls(batch, seq, ctx_len, tile)

    def mod_spec(self, layer, d):
        n_lat, tpb, batch = self.n_lat, self.tiles_per_batch, self.batch
        return pl.BlockSpec(
            (1, 1, N_MOD, d),
            lambda i: (layer, jnp.where(i < n_lat, i // tpb, batch), 0, 0))

    def rope_spec(self, width):
        n_lat, tpb = self.n_lat, self.tiles_per_batch
        return pl.BlockSpec((self.tile, width), lambda i: (jnp.where(i < n_lat, i % tpb, tpb), 0))

    def row_spec(self, width):
        return pl.BlockSpec((self.tile, width), lambda i: (i, 0))


def _mod_rows(mod_ref, base):
    return [mod_ref[0, 0, base + k:base + k + 1, :] for k in range(3)]


def _ffn_kernel(*refs, base, ffn_dim, chunks, final, n_lat_split):
    refs = list(refs)
    o_ref = refs.pop()
    fg_ref = refs.pop() if final else None
    if n_lat_split is None:
        x_ref, mod_ref, g_ref, w1_ref, w2_ref = refs
        xc_ref = None
    else:
        x_ref, xc_ref, mod_ref, g_ref, w1_ref, w2_ref = refs
    shift, scale, gate = _mod_rows(mod_ref, base)
    for r0 in range(0, o_ref.shape[0], ROW_TILE):
        rs = slice(r0, r0 + ROW_TILE)
        x = x_ref[rs, :]
        if xc_ref is not None:
            x = jnp.where(pl.program_id(0) < n_lat_split, x, xc_ref[rs, :])
        h = _modulated(x, g_ref[...], shift, scale).astype(BF16)
        acc = None
        lo = 0
        for width in chunks:
            gt = _dot(h, w1_ref[:, lo:lo + width])
            up = _dot(h, w1_ref[:, ffn_dim + lo:ffn_dim + lo + width])
            a = (_silu(gt) * up).astype(BF16)
            part = _dot(a, w2_ref[lo:lo + width, :])
            acc = part if acc is None else acc + part
            lo += width
        y = x + (0.5 * gate) * acc
        if final:
            y = _rms_rows(y) * fg_ref[...]
        o_ref[rs, :] = y


def _ffn_chunks(ffn_dim):
    assert ffn_dim % MXU_DEPTH == 0
    n_tiles = ffn_dim // MXU_DEPTH
    first = (n_tiles // FFN_SPLIT) * MXU_DEPTH
    sizes = [first] * (FFN_SPLIT - 1) + [ffn_dim - first * (FFN_SPLIT - 1)]
    return tuple(s for s in sizes if s)


def _ffn_call(rows, latent_only, x, mod, layer, base, gain, w1, w2, which, final_g=None,
              x_ctx=None):
    d = x.shape[1]
    ffn_dim = w2.shape[-2]
    chunks = _ffn_chunks(ffn_dim)
    final = final_g is not None
    n_lat = rows.n_lat
    n_tiles = n_lat if latent_only else rows.n_all
    if x_ctx is None:
        in_specs, args = [rows.row_spec(d)], [x]
    else:
        in_specs = [pl.BlockSpec((rows.tile, d), lambda i: (jnp.minimum(i, n_lat - 1), 0)),
                    pl.BlockSpec((rows.tile, d), lambda i: (jnp.maximum(i - n_lat, 0), 0))]
        args = [x, x_ctx]
    in_specs += [rows.mod_spec(layer, d), pl.BlockSpec((1, d), lambda i: (0, 0)),
                 _resident(w1, (layer, which)), _resident(w2, (layer, which))]
    args += [mod, gain.reshape(1, d), w1, w2]
    if final:
        in_specs.append(pl.BlockSpec((1, d), lambda i: (0, 0)))
        args.append(final_g.reshape(1, d))
    return pl.pallas_call(
        functools.partial(_ffn_kernel, base=base, ffn_dim=ffn_dim, chunks=chunks, final=final,
                          n_lat_split=None if x_ctx is None else n_lat),
        grid=(n_tiles,),
        in_specs=in_specs,
        out_specs=rows.row_spec(d),
        out_shape=jax.ShapeDtypeStruct((n_tiles * rows.tile, d), F32),
        compiler_params=_params(1),
        name="ffn",
    )(*args)


def _rope_halves(x1, x2, cos, sin):
    return x1 * cos - x2 * sin, x2 * cos + x1 * sin


def _proj_ret_kernel(x_ref, mod_ref, g_ref, w_ref, cos_ref, sin_ref, o_ref):
    shift, scale, _ = _mod_rows(mod_ref, 3)
    h = _modulated(x_ref[...], g_ref[...], shift, scale).astype(BF16)
    cos, sin = cos_ref[...], sin_ref[...]
    qd = RET_HEADS * RET_DK
    vd = RET_HEADS * RET_DV
    half = RET_DK // 2
    for part, mult in ((0, None), (1, RET_DK ** -0.5)):
        p = _dot(h, w_ref[:, part * qd:(part + 1) * qd])
        if mult is not None:
            p = p * mult
        for hh in range(RET_HEADS):
            lo = hh * RET_DK
            r1, r2 = _rope_halves(p[:, lo:lo + half], p[:, lo + half:lo + RET_DK], cos, sin)
            o_ref[:, part * qd + lo:part * qd + lo + half] = r1.astype(BF16)
            o_ref[:, part * qd + lo + half:part * qd + lo + RET_DK] = r2.astype(BF16)
    for part in range(2):
        lo = 2 * qd + part * vd
        o_ref[:, lo:lo + vd] = _dot(h, w_ref[:, lo:lo + vd]).astype(BF16)


def _proj_att_kernel(x_ref, mod_ref, g_ref, w_ref, qg_ref, kg_ref, cos_ref, sin_ref, o_ref):
    shift, scale, _ = _mod_rows(mod_ref, 3)
    qd = ATT_HEADS * ATT_HD
    kd = ATT_KV_HEADS * ATT_HD
    for r0 in range(0, o_ref.shape[0], ROW_TILE):
        rs = slice(r0, r0 + ROW_TILE)
        h = _modulated(x_ref[rs, :], g_ref[...], shift, scale).astype(BF16)
        cos, sin = cos_ref[rs, :], sin_ref[rs, :]

        def norm_rope(p, n_heads, gain, out_lo):
            for hh in range(n_heads):
                lo = hh * ATT_HD
                y = _rms_rows(p[:, lo:lo + ATT_HD]) * gain
                r = y * cos + pltpu.roll(y, ATT_HD // 2, 1) * sin
                o_ref[rs, out_lo + lo:out_lo + lo + ATT_HD] = r.astype(BF16)

        norm_rope(_dot(h, w_ref[:, :qd]), ATT_HEADS, qg_ref[...], 0)
        norm_rope(_dot(h, w_ref[:, qd:qd + kd]), ATT_KV_HEADS, kg_ref[...], qd)
        o_ref[rs, qd + kd:] = _dot(h, w_ref[:, qd + kd:]).astype(BF16)


def _proj_call(kernel, name, rows, x, mod, layer, gain, w, which, extras, extra_specs):
    d = x.shape[1]
    n_out = w.shape[-1]
    return pl.pallas_call(
        kernel,
        grid=(rows.n_all,),
        in_specs=[rows.row_spec(d), rows.mod_spec(layer, d), pl.BlockSpec((1, d), lambda i: (0, 0)),
                  _resident(w, (which,))] + extra_specs,
        out_specs=rows.row_spec(n_out),
        out_shape=jax.ShapeDtypeStruct((rows.n_all * rows.tile, n_out), BF16),
        compiler_params=_params(1),
        name=name,
    )(x, mod, gain.reshape(1, d), w, *extras)


def _decay_tables(n, lg_f, lg_b):
    i = lax.broadcasted_iota(jnp.int32, (n, n), 0)
    j = lax.broadcasted_iota(jnp.int32, (n, n), 1)
    diff = (i - j).astype(F32)
    mask = (jnp.where(diff >= 0, jnp.exp(jnp.maximum(diff, 0.0) * lg_f), 0.0)
            + jnp.where(diff <= 0, jnp.exp(jnp.maximum(-diff, 0.0) * lg_b), 0.0))
    pos = lax.broadcasted_iota(jnp.int32, (n, 1), 0).astype(F32)
    xi_f = jnp.exp((pos + 1.0) * lg_f)
    xi_b = jnp.exp((n - pos) * lg_b)
    zeta_f = jnp.exp((n - 1.0 - pos) * lg_f)
    zeta_b = jnp.exp(pos * lg_b)
    return mask, xi_f, xi_b, zeta_f, zeta_b


def _scaled_bf16(x, col):
    return (x.astype(F32) * col).astype(BF16)


def _head_norm_gate(o, g):
    return (_rms_rows(o) * _silu(g.astype(F32))).astype(BF16)


def _scaled_pair(x, col_a, col_b):
    xf = x.astype(F32)
    return jnp.concatenate([(xf * col_a).astype(BF16), (xf * col_b).astype(BF16)], axis=1)


def _ret_kernel(ql_ref, kl_ref, vl_ref, gl_ref, qc_ref, kc_ref, vc_ref, gc_ref, dec_ref,
                ol_ref, oc_ref, st_ref, ub_ref, runf_ref, runb_ref, mask_ref, cmask_ref,
                col_ref, ccol_ref, *, n_chunks, chunk, ctx_len):
    dk = RET_DK
    lg_f = -jnp.exp(dec_ref[0, 0:1, 0:1])
    lg_b = -jnp.exp(dec_ref[0, 1:2, 0:1])

    @pl.when(pl.program_id(1) == 0)
    def _():
        mask, xi_f, xi_b, zeta_f, zeta_b = _decay_tables(chunk, lg_f, lg_b)
        mask_ref[...] = mask
        for k, col in enumerate((xi_f, xi_b, zeta_f, zeta_b)):
            col_ref[k] = col
        cmask, _, _, czeta_f, czeta_b = _decay_tables(ctx_len, lg_f, lg_b)
        cmask_ref[...] = cmask
        ccol_ref[0] = czeta_f
        ccol_ref[1] = czeta_b

    decay_f = jnp.exp(chunk * lg_f)
    decay_b = jnp.exp(chunk * lg_b)

    q, k, v = qc_ref[...], kc_ref[...], vc_ref[...]
    p = (_dot_nt(q, k) * cmask_ref[...]).astype(BF16)
    oc_ref[...] = _head_norm_gate(_dot(p, v), gc_ref[...])
    s0 = _dot_tn(_scaled_pair(k, ccol_ref[0], ccol_ref[1]), v)
    runf_ref[...] = s0[:dk]
    runb_ref[...] = s0[dk:]
    st_ref[0, :dk, :] = s0[:dk].astype(BF16)
    st_ref[n_chunks - 1, dk:, :] = s0[dk:].astype(BF16)

    def rows(n):
        return slice(n * chunk, (n + 1) * chunk)

    for n in range(n_chunks):
        u = _dot_tn(_scaled_pair(kl_ref[rows(n), :], col_ref[2], col_ref[3]), vl_ref[rows(n), :])
        if n + 1 < n_chunks:
            s = runf_ref[...] * decay_f + u[:dk]
            runf_ref[...] = s
            st_ref[n + 1, :dk, :] = s.astype(BF16)
        if n > 0:
            ub_ref[n] = u[dk:]

    for n in range(n_chunks - 1, 0, -1):
        s = runb_ref[...] * decay_b + ub_ref[n]
        runb_ref[...] = s
        st_ref[n - 1, dk:, :] = s.astype(BF16)

    for n in range(n_chunks):
        r = rows(n)
        q, k, v = ql_ref[r, :], kl_ref[r, :], vl_ref[r, :]
        p = (_dot_nt(q, k) * mask_ref[...]).astype(BF16)
        o = _dot(p, v) + _dot(_scaled_pair(q, col_ref[0], col_ref[1]), st_ref[n])
        ol_ref[r, :] = _head_norm_gate(o, gl_ref[r, :])


def _ret_call(batch, seq, ctx_len, proj, dec_tab):
    chunk = min(RET_CHUNK, seq)
    assert seq % chunk == 0
    n_chunks = seq // chunk
    ctx_row0 = batch * seq // ctx_len
    kq = RET_HEADS
    kv = 2 * RET_HEADS * RET_DK // RET_DV
    kg = kv + RET_HEADS
    vd = RET_HEADS * RET_DV
    in_specs = [
        pl.BlockSpec((seq, RET_DK), lambda h, b: (b, h)),
        pl.BlockSpec((seq, RET_DK), lambda h, b: (b, kq + h)),
        pl.BlockSpec((seq, RET_DV), lambda h, b: (b, kv + h)),
        pl.BlockSpec((seq, RET_DV), lambda h, b: (b, kg + h)),
        pl.BlockSpec((ctx_len, RET_DK), lambda h, b: (ctx_row0 + b, h)),
        pl.BlockSpec((ctx_len, RET_DK), lambda h, b: (ctx_row0 + b, kq + h)),
        pl.BlockSpec((ctx_len, RET_DV), lambda h, b: (ctx_row0 + b, kv + h)),
        pl.BlockSpec((ctx_len, RET_DV), lambda h, b: (ctx_row0 + b, kg + h)),
        pl.BlockSpec((1, 8, 128), lambda h, b: (h, 0, 0)),
    ]
    return pl.pallas_call(
        functools.partial(_ret_kernel, n_chunks=n_chunks, chunk=chunk, ctx_len=ctx_len),
        grid=(RET_HEADS, batch),
        in_specs=in_specs,
        out_specs=[pl.BlockSpec((seq, RET_DV), lambda h, b: (b, h)),
                   pl.BlockSpec((ctx_len, RET_DV), lambda h, b: (b, h))],
        out_shape=[jax.ShapeDtypeStruct((batch * seq, vd), BF16),
                   jax.ShapeDtypeStruct((batch * ctx_len, vd), BF16)],
        scratch_shapes=[pltpu.VMEM((n_chunks, 2 * RET_DK, RET_DV), BF16),
                        pltpu.VMEM((n_chunks, RET_DK, RET_DV), F32),
                        pltpu.VMEM((RET_DK, RET_DV), F32),
                        pltpu.VMEM((RET_DK, RET_DV), F32),
                        pltpu.VMEM((chunk, chunk), F32),
                        pltpu.VMEM((ctx_len, ctx_len), F32),
                        pltpu.VMEM((4, chunk, 1), F32),
                        pltpu.VMEM((2, ctx_len, 1), F32)],
        compiler_params=_params(2),
        name="ret",
    )(proj, proj, proj, proj, proj, proj, proj, proj, dec_tab)


def _att_kernel(q_ref, kl_ref, vl_ref, kc_ref, vc_ref, o_ref, s_ref, p_ref, *, n_q, seq):
    qi = pl.program_id(2)
    group = ATT_HEADS // ATT_KV_HEADS
    tq = q_ref.shape[0]
    n_keys = s_ref.shape[1]
    exp2_scale = ATT_HD ** -0.5 * 1.4426950408889634

    def attend(col_lo):
        slabs = range(col_lo, n_keys, LANES)
        row_max = []
        for g in range(group):
            q = q_ref[:, g * ATT_HD:(g + 1) * ATT_HD]
            rs = slice(g * tq, (g + 1) * tq)
            m = None
            for k0 in range(col_lo, n_keys, ATT_KEY_CHUNK):
                k1 = min(k0 + ATT_KEY_CHUNK, seq if k0 < seq else n_keys)
                keys = kl_ref[k0:k1, :] if k0 < seq else kc_ref[k0 - seq:k1 - seq, :]
                sc = _dot_nt(q, keys)
                s_ref[rs, k0:k1] = sc
                for c0 in range(0, k1 - k0, LANES):
                    t = sc[:, c0:c0 + LANES]
                    m = t if m is None else jnp.maximum(m, t)
            row_max.append(jnp.max(m, axis=-1, keepdims=True))
        for g in range(group):
            rs = slice(g * tq, (g + 1) * tq)
            m = row_max[g]
            acc = None
            o = None
            for k0 in range(col_lo, n_keys, ATT_KEY_CHUNK):
                k1 = min(k0 + ATT_KEY_CHUNK, seq if k0 < seq else n_keys)
                for c0 in range(k0, k1, LANES):
                    e = jnp.exp2((s_ref[rs, c0:c0 + LANES] - m) * exp2_scale)
                    acc = e if acc is None else acc + e
                    p_ref[rs, c0:c0 + LANES] = e.astype(BF16)
                v = vl_ref[k0:k1, :] if k0 < seq else vc_ref[k0 - seq:k1 - seq, :]
                part = _dot(p_ref[rs, k0:k1], v)
                o = part if o is None else o + part
            o = jnp.where(pl.program_id(1) == 0, o[:, :ATT_HD], o[:, ATT_HD:])
            o = o / jnp.sum(acc, axis=-1, keepdims=True)
            o_ref[:, g * ATT_HD:(g + 1) * ATT_HD] = o.astype(BF16)

    @pl.when(qi < n_q)
    def _():
        attend(0)

    @pl.when(qi >= n_q)
    def _():
        attend(seq)


def _att_call(batch, seq, ctx_len, proj, ctx_out):
    tq = ATT_Q_TILE
    assert ctx_len == tq and seq % tq == 0
    n_q = seq // tq
    ctx_row0 = batch * seq // ctx_len
    group_w = (ATT_HEADS // ATT_KV_HEADS) * ATT_HD
    k_col = ATT_HEADS
    v_col = ATT_HEADS + ATT_KV_HEADS
    n_rows = batch * seq + (batch * ctx_len if ctx_out else 0)

    def q_map(b, kh, qi):
        return (jnp.where(qi < n_q, b * n_q + qi, ctx_row0 + b), kh)

    return pl.pallas_call(
        functools.partial(_att_kernel, n_q=n_q, seq=seq),
        grid=(batch, ATT_KV_HEADS, n_q + (1 if ctx_out else 0)),
        scratch_shapes=[pltpu.VMEM((group_w // ATT_HD * tq, seq + ctx_len), F32),
                        pltpu.VMEM((group_w // ATT_HD * tq, seq + ctx_len), BF16)],
        in_specs=[
            pl.BlockSpec((tq, group_w), q_map),
            pl.BlockSpec((seq, ATT_HD), lambda b, kh, qi: (b, k_col + kh)),
            pl.BlockSpec((seq, 2 * ATT_HD), lambda b, kh, qi: (b, v_col // 2)),
            pl.BlockSpec((ctx_len, ATT_HD), lambda b, kh, qi: (ctx_row0 + b, k_col + kh)),
            pl.BlockSpec((ctx_len, 2 * ATT_HD), lambda b, kh, qi: (ctx_row0 + b, v_col // 2)),
        ],
        out_specs=pl.BlockSpec((tq, group_w), q_map),
        out_shape=jax.ShapeDtypeStruct((n_rows, ATT_HEADS * ATT_HD), BF16),
        compiler_params=_params(3),
        name="att",
    )(proj, proj, proj, proj, proj)


def _oproj_kernel(x_ref, mod_ref, w_ref, *rest, n_lat):
    gate = mod_ref[0, 0, 5:6, :]
    if len(rest) == 2:
        a_ref, o_ref = rest
        o_ref[...] = x_ref[...] + gate * _dot(a_ref[...], w_ref[...])
        return
    al_ref, ac_ref, o_ref = rest
    i = pl.program_id(0)

    @pl.when(i < n_lat)
    def _():
        o_ref[...] = x_ref[...] + gate * _dot(al_ref[...], w_ref[...])

    @pl.when(i >= n_lat)
    def _():
        o_ref[...] = x_ref[...] + gate * _dot(ac_ref[...], w_ref[...])


def _oproj_call(rows, n_tiles, x, mod, layer, w, which, a_lat, a_ctx=None):
    d = x.shape[1]
    k = w.shape[-2]
    n_lat = rows.n_lat
    in_specs = [rows.row_spec(d), rows.mod_spec(layer, d), _resident(w, (which,))]
    args = [x, mod, w]
    if a_ctx is None:
        in_specs.append(rows.row_spec(k))
        args.append(a_lat)
    else:
        in_specs.append(pl.BlockSpec((ROW_TILE, k), lambda i: (jnp.minimum(i, n_lat - 1), 0)))
        in_specs.append(pl.BlockSpec((ROW_TILE, k), lambda i: (jnp.maximum(i - n_lat, 0), 0)))
        args += [a_lat, a_ctx]
    return pl.pallas_call(
        functools.partial(_oproj_kernel, n_lat=n_lat),
        grid=(n_tiles,),
        in_specs=in_specs,
        out_specs=rows.row_spec(d),
        out_shape=jax.ShapeDtypeStruct((n_tiles * ROW_TILE, d), F32),
        compiler_params=_params(1),
        name="oproj",
    )(*args)


def _with_identity_rows(cos, sin, pad_rows):
    pad = (pad_rows, cos.shape[1])
    return (jnp.concatenate([cos, jnp.ones(pad, F32)], axis=0),
            jnp.concatenate([sin, jnp.zeros(pad, F32)], axis=0))


def _seq_rope_tables(seq, pad_rows):
    half = RET_DK // 2
    freqs = ROPE_THETA ** (-jnp.arange(half, dtype=F32) / half)
    ang = jnp.arange(seq, dtype=F32)[:, None] * freqs
    return _with_identity_rows(jnp.cos(ang), jnp.sin(ang), pad_rows)


def _axial_rope_tables(seq, pad_rows):
    quarter = ATT_HD // 4
    tok = jnp.arange(seq)
    r = (tok // GRID_W).astype(F32)
    cl = (tok % GRID_W).astype(F32)
    freqs = ROPE_THETA ** (-jnp.arange(quarter, dtype=F32) / quarter)
    ang = jnp.concatenate([r[:, None] * freqs, cl[:, None] * freqs], axis=-1)
    cos, sin = jnp.cos(ang), jnp.sin(ang)
    return _with_identity_rows(jnp.concatenate([cos, cos], axis=-1),
                               jnp.concatenate([-sin, sin], axis=-1), pad_rows)


def kernel(x, c, ctx, c_ctx, ada_w, ada_b, norm_g, ffn_w1, ffn_w2, ret_w_in, ret_w_out,
           ret_decay_f, ret_decay_b, att_w_qkv, att_w_o, att_q_gain, att_k_gain, final_g):
    batch, seq, d = x.shape
    ctx_len = ctx.shape[1]
    depth = ada_w.shape[0]
    rows = _Rows(batch, seq, ctx_len)
    rows_ffn = _Rows.widest(batch, seq, ctx_len, FFN_SUBTILES)

    cc = jnp.concatenate(
        [c, c_ctx[None], jnp.zeros((ADA_ROWS - batch - 1, d), F32)], axis=0)
    mod = _ada_call(cc, ada_w, ada_b).reshape(depth, ADA_ROWS, N_MOD, d)

    w1 = ffn_w1.astype(BF16)
    w2 = ffn_w2.astype(BF16)
    w_in = ret_w_in.astype(BF16)
    w_out = ret_w_out.astype(BF16)
    w_qkv = att_w_qkv.astype(BF16)
    w_o = att_w_o.astype(BF16)

    rows_att = _Rows.widest(batch, seq, ctx_len, PROJ_ATT_SUBTILES)
    ret_cos, ret_sin = _seq_rope_tables(seq, rows.tile)
    att_cos, att_sin = _axial_rope_tables(seq, rows_att.tile)
    ret_rope_specs = [rows.rope_spec(RET_DK // 2)] * 2
    att_rope_specs = [rows_att.rope_spec(ATT_HD)] * 2
    gain_spec = pl.BlockSpec((1, ATT_HD), lambda i: (0, 0))

    xs = x.reshape(batch * seq, d)
    xs_ctx = ctx.reshape(batch * ctx_len, d)

    for i in range(depth):
        last = i == depth - 1
        j = i // 2
        xs = _ffn_call(rows_ffn, False, xs, mod, i, 0, norm_g[i, 0], w1, w2, 0,
                       x_ctx=xs_ctx if i == 0 else None)
        n_tiles = rows.n_lat if last else rows.n_all
        if i % 2 == 0:
            proj = _proj_call(_proj_ret_kernel, "proj_ret", rows, xs, mod, i, norm_g[i, 1],
                              w_in, j, [ret_cos, ret_sin], ret_rope_specs)
            dec_tab = jnp.broadcast_to(
                jnp.stack([ret_decay_f[j], ret_decay_b[j]], axis=1)[:, :, None],
                (RET_HEADS, 2, 128))
            dec_tab = jnp.concatenate([dec_tab, jnp.zeros((RET_HEADS, 6, 128), F32)], axis=1)
            a_lat, a_ctx = _ret_call(batch, seq, ctx_len, proj, dec_tab)
            xs = _oproj_call(rows, n_tiles, xs, mod, i, w_out, j, a_lat,
                             None if last else a_ctx)
        else:
            proj = _proj_call(_proj_att_kernel, "proj_att", rows_att, xs, mod, i, norm_g[i, 1],
                              w_qkv, j,
                              [att_q_gain[j].reshape(1, ATT_HD), att_k_gain[j].reshape(1, ATT_HD),
                               att_cos, att_sin],
                              [gain_spec, gain_spec] + att_rope_specs)
            a = _att_call(batch, seq, ctx_len, proj, not last)
            xs = _oproj_call(rows, n_tiles, xs, mod, i, w_o, j, a)
        xs = _ffn_call(rows_ffn, last, xs, mod, i, 6, norm_g[i, 2], w1, w2, 1,
                       final_g if last else None)
    return xs.reshape(batch, seq, d)
```

```python
import functools

import jax
import jax.numpy as jnp
from jax import lax
from jax.experimental import pallas as pl
from jax.experimental.pallas import tpu as pltpu

F32 = jnp.float32
BF16 = jnp.bfloat16

N_MOD = 9
RET_HEADS = 4
RET_DK = 256
RET_DV = 512
ATT_HEADS = 8
ATT_KV_HEADS = 2
ATT_HD = 128
GRID_W = 64
ROPE_THETA = 10000.0
EPS = 1e-6

ROW_TILE = 512
RET_CHUNK = 512
ATT_Q_TILE = 256
ATT_KEY_CHUNK = 512
LANES = 128
MXU_DEPTH = 256
FFN_SPLIT = 2
FFN_SUBTILES = 2
PROJ_ATT_SUBTILES = 2
ADA_ROWS = 16
VMEM_LIMIT = 56 * 1024 * 1024


def _params(n_axes):
    return pltpu.CompilerParams(
        dimension_semantics=("arbitrary",) * n_axes, vmem_limit_bytes=VMEM_LIMIT)


def _resident(stacked, lead):
    tail = stacked.shape[len(lead):]
    index = tuple(lead) + (0,) * len(tail)
    return pl.BlockSpec((None,) * len(lead) + tail, lambda *_: index,
                        pipeline_mode=pl.Buffered(1))


def _silu(x):
    return x * jax.nn.sigmoid(x)


def _rms_rows(x):
    return x * lax.rsqrt(jnp.mean(x * x, axis=-1, keepdims=True) + EPS)


def _modulated(x, gain, shift, scale):
    return (_rms_rows(x) * gain) * (1.0 + scale) + shift


def _dot(a, b):
    return jnp.dot(a, b, preferred_element_type=F32)


def _dot_nt(a, b):
    return lax.dot_general(a, b, (((1,), (1,)), ((), ())), preferred_element_type=F32)


def _dot_tn(a, b):
    return lax.dot_general(a, b, (((0,), (0,)), ((), ())), preferred_element_type=F32)


def _ada_kernel(cc_ref, w_ref, b_ref, o_ref):
    s = _silu(cc_ref[...])
    o_ref[0] = _dot(s.astype(BF16), w_ref[0].astype(BF16)) + b_ref[0]


def _ada_call(cc, ada_w, ada_b):
    depth, d, n = ada_w.shape
    tn = 1024
    return pl.pallas_call(
        _ada_kernel,
        grid=(depth, n // tn),
        in_specs=[
            pl.BlockSpec((ADA_ROWS, d), lambda l, j: (0, 0)),
            pl.BlockSpec((1, d, tn), lambda l, j: (l, 0, j)),
            pl.BlockSpec((1, 1, tn), lambda l, j: (l, 0, j)),
        ],
        out_specs=pl.BlockSpec((1, ADA_ROWS, tn), lambda l, j: (l, 0, j)),
        out_shape=jax.ShapeDtypeStruct((depth, ADA_ROWS, n), F32),
        compiler_params=_params(2),
        name="ada",
    )(cc, ada_w, ada_b.reshape(depth, 1, n))


class _Rows:
    def __init__(self, batch, seq, ctx_len, tile=ROW_TILE):
        self.batch, self.seq, self.ctx_len, self.tile = batch, seq, ctx_len, tile
        assert seq % tile == 0 and (batch * ctx_len) % tile == 0
        self.tiles_per_batch = seq // tile
        self.n_lat = batch * seq // tile
        self.n_ctx = batch * ctx_len // tile
        self.n_all = self.n_lat + self.n_ctx

    @classmethod
    def widest(cls, batch, seq, ctx_len, factor):
        tile = ROW_TILE * factor
        if seq % tile or (batch * ctx_len) % tile:
            tile = ROW_TILE
        return cls(batch, seq, ctx_len, tile)

    def mod_spec(self, layer, d):
        n_lat, tpb, batch = self.n_lat, self.tiles_per_batch, self.batch
        return pl.BlockSpec(
            (1, 1, N_MOD, d),
            lambda i: (layer, jnp.where(i < n_lat, i // tpb, batch), 0, 0))

    def rope_spec(self, width):
        n_lat, tpb = self.n_lat, self.tiles_per_batch
        return pl.BlockSpec((self.tile, width), lambda i: (jnp.where(i < n_lat, i % tpb, tpb), 0))

    def row_spec(self, width):
        return pl.BlockSpec((self.tile, width), lambda i: (i, 0))


def _mod_rows(mod_ref, base):
    return [mod_ref[0, 0, base + k:base + k + 1, :] for k in range(3)]


def _ffn_kernel(*refs, base, ffn_dim, chunks, final, n_lat_split):
    refs = list(refs)
    o_ref = refs.pop()
    fg_ref = refs.pop() if final else None
    if n_lat_split is None:
        x_ref, mod_ref, g_ref, w1_ref, w2_ref = refs
        xc_ref = None
    else:
        x_ref, xc_ref, mod_ref, g_ref, w1_ref, w2_ref = refs
    shift, scale, gate = _mod_rows(mod_ref, base)
    for r0 in range(0, o_ref.shape[0], ROW_TILE):
        rs = slice(r0, r0 + ROW_TILE)
        x = x_ref[rs, :]
        if xc_ref is not None:
            x = jnp.where(pl.program_id(0) < n_lat_split, x, xc_ref[rs, :])
        h = _modulated(x, g_ref[...], shift, scale).astype(BF16)
        acc = None
        lo = 0
        for width in chunks:
            gt = _dot(h, w1_ref[:, lo:lo + width])
            up = _dot(h, w1_ref[:, ffn_dim + lo:ffn_dim + lo + width])
            a = (_silu(gt) * up).astype(BF16)
            part = _dot(a, w2_ref[lo:lo + width, :])
            acc = part if acc is None else acc + part
            lo += width
        y = x + (0.5 * gate) * acc
        if final:
            y = _rms_rows(y) * fg_ref[...]
        o_ref[rs, :] = y


def _ffn_chunks(ffn_dim):
    assert ffn_dim % MXU_DEPTH == 0
    n_tiles = ffn_dim // MXU_DEPTH
    first = (n_tiles // FFN_SPLIT) * MXU_DEPTH
    sizes = [first] * (FFN_SPLIT - 1) + [ffn_dim - first * (FFN_SPLIT - 1)]
    return tuple(s for s in sizes if s)


def _ffn_call(rows, latent_only, x, mod, layer, base, gain, w1, w2, which, final_g=None,
              x_ctx=None):
    d = x.shape[1]
    ffn_dim = w2.shape[-2]
    chunks = _ffn_chunks(ffn_dim)
    final = final_g is not None
    n_lat = rows.n_lat
    n_tiles = n_lat if latent_only else rows.n_all
    if x_ctx is None:
        in_specs, args = [rows.row_spec(d)], [x]
    else:
        in_specs = [pl.BlockSpec((rows.tile, d), lambda i: (jnp.minimum(i, n_lat - 1), 0)),
                    pl.BlockSpec((rows.tile, d), lambda i: (jnp.maximum(i - n_lat, 0), 0))]
        args = [x, x_ctx]
    in_specs += [rows.mod_spec(layer, d), pl.BlockSpec((1, d), lambda i: (0, 0)),
                 _resident(w1, (layer, which)), _resident(w2, (layer, which))]
    args += [mod, gain.reshape(1, d), w1, w2]
    if final:
        in_specs.append(pl.BlockSpec((1, d), lambda i: (0, 0)))
        args.append(final_g.reshape(1, d))
    return pl.pallas_call(
        functools.partial(_ffn_kernel, base=base, ffn_dim=ffn_dim, chunks=chunks, final=final,
                          n_lat_split=None if x_ctx is None else n_lat),
        grid=(n_tiles,),
        in_specs=in_specs,
        out_specs=rows.row_spec(d),
        out_shape=jax.ShapeDtypeStruct((n_tiles * rows.tile, d), F32),
        compiler_params=_params(1),
        name="ffn",
    )(*args)


def _rope_halves(x1, x2, cos, sin):
    return x1 * cos - x2 * sin, x2 * cos + x1 * sin


def _proj_ret_kernel(x_ref, mod_ref, g_ref, w_ref, cos_ref, sin_ref, o_ref):
    shift, scale, _ = _mod_rows(mod_ref, 3)
    h = _modulated(x_ref[...], g_ref[...], shift, scale).astype(BF16)
    cos, sin = cos_ref[...], sin_ref[...]
    qd = RET_HEADS * RET_DK
    vd = RET_HEADS * RET_DV
    half = RET_DK // 2
    for part, mult in ((0, None), (1, RET_DK ** -0.5)):
        p = _dot(h, w_ref[:, part * qd:(part + 1) * qd])
        if mult is not None:
            p = p * mult
        for hh in range(RET_HEADS):
            lo = hh * RET_DK
            r1, r2 = _rope_halves(p[:, lo:lo + half], p[:, lo + half:lo + RET_DK], cos, sin)
            o_ref[:, part * qd + lo:part * qd + lo + half] = r1.astype(BF16)
            o_ref[:, part * qd + lo + half:part * qd + lo + RET_DK] = r2.astype(BF16)
    for part in range(2):
        lo = 2 * qd + part * vd
        o_ref[:, lo:lo + vd] = _dot(h, w_ref[:, lo:lo + vd]).astype(BF16)


def _proj_att_kernel(x_ref, mod_ref, g_ref, w_ref, qg_ref, kg_ref, cos_ref, sin_ref, o_ref):
    shift, scale, _ = _mod_rows(mod_ref, 3)
    qd = ATT_HEADS * ATT_HD
    kd = ATT_KV_HEADS * ATT_HD
    for r0 in range(0, o_ref.shape[0], ROW_TILE):
        rs = slice(r0, r0 + ROW_TILE)
        h = _modulated(x_ref[rs, :], g_ref[...], shift, scale).astype(BF16)
        cos, sin = cos_ref[rs, :], sin_ref[rs, :]

        def norm_rope(p, n_heads, gain, out_lo):
            for hh in range(n_heads):
                lo = hh * ATT_HD
                y = _rms_rows(p[:, lo:lo + ATT_HD]) * gain
                r = y * cos + pltpu.roll(y, ATT_HD // 2, 1) * sin
                o_ref[rs, out_lo + lo:out_lo + lo + ATT_HD] = r.astype(BF16)

        norm_rope(_dot(h, w_ref[:, :qd]), ATT_HEADS, qg_ref[...], 0)
        norm_rope(_dot(h, w_ref[:, qd:qd + kd]), ATT_KV_HEADS, kg_ref[...], qd)
        o_ref[rs, qd + kd:] = _dot(h, w_ref[:, qd + kd:]).astype(BF16)


def _proj_call(kernel, name, rows, x, mod, layer, gain, w, which, extras, extra_specs):
    d = x.shape[1]
    n_out = w.shape[-1]
    return pl.pallas_call(
        kernel,
        grid=(rows.n_all,),
        in_specs=[rows.row_spec(d), rows.mod_spec(layer, d), pl.BlockSpec((1, d), lambda i: (0, 0)),
                  _resident(w, (which,))] + extra_specs,
        out_specs=rows.row_spec(n_out),
        out_shape=jax.ShapeDtypeStruct((rows.n_all * rows.tile, n_out), BF16),
        compiler_params=_params(1),
        name=name,
    )(x, mod, gain.reshape(1, d), w, *extras)


def _decay_tables(n, lg_f, lg_b):
    i = lax.broadcasted_iota(jnp.int32, (n, n), 0)
    j = lax.broadcasted_iota(jnp.int32, (n, n), 1)
    diff = (i - j).astype(F32)
    mask = (jnp.where(diff >= 0, jnp.exp(jnp.maximum(diff, 0.0) * lg_f), 0.0)
            + jnp.where(diff <= 0, jnp.exp(jnp.maximum(-diff, 0.0) * lg_b), 0.0))
    pos = lax.broadcasted_iota(jnp.int32, (n, 1), 0).astype(F32)
    xi_f = jnp.exp((pos + 1.0) * lg_f)
    xi_b = jnp.exp((n - pos) * lg_b)
    zeta_f = jnp.exp((n - 1.0 - pos) * lg_f)
    zeta_b = jnp.exp(pos * lg_b)
    return mask, xi_f, xi_b, zeta_f, zeta_b


def _scaled_bf16(x, col):
    return (x.astype(F32) * col).astype(BF16)


def _head_norm_gate(o, g):
    return (_rms_rows(o) * _silu(g.astype(F32))).astype(BF16)


def _scaled_pair(x, col_a, col_b):
    xf = x.astype(F32)
    return jnp.concatenate([(xf * col_a).astype(BF16), (xf * col_b).astype(BF16)], axis=1)


def _ret_kernel(ql_ref, kl_ref, vl_ref, gl_ref, qc_ref, kc_ref, vc_ref, gc_ref, dec_ref,
                ol_ref, oc_ref, st_ref, ub_ref, runf_ref, runb_ref, mask_ref, cmask_ref,
                col_ref, ccol_ref, *, n_chunks, chunk, ctx_len):
    dk = RET_DK
    lg_f = -jnp.exp(dec_ref[0, 0:1, 0:1])
    lg_b = -jnp.exp(dec_ref[0, 1:2, 0:1])

    @pl.when(pl.program_id(1) == 0)
    def _():
        mask, xi_f, xi_b, zeta_f, zeta_b = _decay_tables(chunk, lg_f, lg_b)
        mask_ref[...] = mask
        for k, col in enumerate((xi_f, xi_b, zeta_f, zeta_b)):
            col_ref[k] = col
        cmask, _, _, czeta_f, czeta_b = _decay_tables(ctx_len, lg_f, lg_b)
        cmask_ref[...] = cmask
        ccol_ref[0] = czeta_f
        ccol_ref[1] = czeta_b

    decay_f = jnp.exp(chunk * lg_f)
    decay_b = jnp.exp(chunk * lg_b)

    q, k, v = qc_ref[...], kc_ref[...], vc_ref[...]
    p = (_dot_nt(q, k) * cmask_ref[...]).astype(BF16)
    oc_ref[...] = _head_norm_gate(_dot(p, v), gc_ref[...])
    s0 = _dot_tn(_scaled_pair(k, ccol_ref[0], ccol_ref[1]), v)
    runf_ref[...] = s0[:dk]
    runb_ref[...] = s0[dk:]
    st_ref[0, :dk, :] = s0[:dk].astype(BF16)
    st_ref[n_chunks - 1, dk:, :] = s0[dk:].astype(BF16)

    def rows(n):
        return slice(n * chunk, (n + 1) * chunk)

    for n in range(n_chunks):
        u = _dot_tn(_scaled_pair(kl_ref[rows(n), :], col_ref[2], col_ref[3]), vl_ref[rows(n), :])
        if n + 1 < n_chunks:
            s = runf_ref[...] * decay_f + u[:dk]
            runf_ref[...] = s
            st_ref[n + 1, :dk, :] = s.astype(BF16)
        if n > 0:
            ub_ref[n] = u[dk:]

    for n in range(n_chunks - 1, 0, -1):
        s = runb_ref[...] * decay_b + ub_ref[n]
        runb_ref[...] = s
        st_ref[n - 1, dk:, :] = s.astype(BF16)

    for n in range(n_chunks):
        r = rows(n)
        q, k, v = ql_ref[r, :], kl_ref[r, :], vl_ref[r, :]
        p = (_dot_nt(q, k) * mask_ref[...]).astype(BF16)
        o = _dot(p, v) + _dot(_scaled_pair(q, col_ref[0], col_ref[1]), st_ref[n])
        ol_ref[r, :] = _head_norm_gate(o, gl_ref[r, :])


def _ret_call(batch, seq, ctx_len, proj, dec_tab):
    chunk = min(RET_CHUNK, seq)
    assert seq % chunk == 0
    n_chunks = seq // chunk
    ctx_row0 = batch * seq // ctx_len
    kq = RET_HEADS
    kv = 2 * RET_HEADS * RET_DK // RET_DV
    kg = kv + RET_HEADS
    vd = RET_HEADS * RET_DV
    in_specs = [
        pl.BlockSpec((seq, RET_DK), lambda h, b: (b, h)),
        pl.BlockSpec((seq, RET_DK), lambda h, b: (b, kq + h)),
        pl.BlockSpec((seq, RET_DV), lambda h, b: (b, kv + h)),
        pl.BlockSpec((seq, RET_DV), lambda h, b: (b, kg + h)),
        pl.BlockSpec((ctx_len, RET_DK), lambda h, b: (ctx_row0 + b, h)),
        pl.BlockSpec((ctx_len, RET_DK), lambda h, b: (ctx_row0 + b, kq + h)),
        pl.BlockSpec((ctx_len, RET_DV), lambda h, b: (ctx_row0 + b, kv + h)),
        pl.BlockSpec((ctx_len, RET_DV), lambda h, b: (ctx_row0 + b, kg + h)),
        pl.BlockSpec((1, 8, 128), lambda h, b: (h, 0, 0)),
    ]
    return pl.pallas_call(
        functools.partial(_ret_kernel, n_chunks=n_chunks, chunk=chunk, ctx_len=ctx_len),
        grid=(RET_HEADS, batch),
        in_specs=in_specs,
        out_specs=[pl.BlockSpec((seq, RET_DV), lambda h, b: (b, h)),
                   pl.BlockSpec((ctx_len, RET_DV), lambda h, b: (b, h))],
        out_shape=[jax.ShapeDtypeStruct((batch * seq, vd), BF16),
                   jax.ShapeDtypeStruct((batch * ctx_len, vd), BF16)],
        scratch_shapes=[pltpu.VMEM((n_chunks, 2 * RET_DK, RET_DV), BF16),
                        pltpu.VMEM((n_chunks, RET_DK, RET_DV), F32),
                        pltpu.VMEM((RET_DK, RET_DV), F32),
                        pltpu.VMEM((RET_DK, RET_DV), F32),
                        pltpu.VMEM((chunk, chunk), F32),
                        pltpu.VMEM((ctx_len, ctx_len), F32),
                        pltpu.VMEM((4, chunk, 1), F32),
                        pltpu.VMEM((2, ctx_len, 1), F32)],
        compiler_params=_params(2),
        name="ret",
    )(proj, proj, proj, proj, proj, proj, proj, proj, dec_tab)


def _att_kernel(q_ref, kl_ref, vl_ref, kc_ref, vc_ref, o_ref, s_ref, p_ref, va_ref, *, n_q, seq):
    qi = pl.program_id(2)
    group = ATT_HEADS // ATT_KV_HEADS
    tq = q_ref.shape[0]
    n_keys = s_ref.shape[1]
    exp2_scale = ATT_HD ** -0.5 * 1.4426950408889634

    @pl.when(qi == 0)
    def _():
        va_ref[:seq, :ATT_HD] = vl_ref[...]
        va_ref[seq:, :ATT_HD] = vc_ref[...]
        va_ref[:, ATT_HD:] = jnp.ones((n_keys, ATT_HD), BF16)

    def attend(col_lo):
        slabs = range(col_lo, n_keys, LANES)
        for g in range(group):
            q = q_ref[:, g * ATT_HD:(g + 1) * ATT_HD]
            if col_lo < seq:
                s_ref[g * tq:(g + 1) * tq, :seq] = _dot_nt(q, kl_ref[...])
            s_ref[g * tq:(g + 1) * tq, seq:] = _dot_nt(q, kc_ref[...])
        for g in range(group):
            rs = slice(g * tq, (g + 1) * tq)
            m = None
            for c0 in slabs:
                t = s_ref[rs, c0:c0 + LANES]
                m = t if m is None else jnp.maximum(m, t)
            m = jnp.max(m, axis=-1, keepdims=True)
            o = None
            for k0 in range(col_lo, n_keys, ATT_KEY_CHUNK):
                k1 = min(k0 + ATT_KEY_CHUNK, seq if k0 < seq else n_keys)
                for c0 in range(k0, k1, LANES):
                    e = jnp.exp2((s_ref[rs, c0:c0 + LANES] - m) * exp2_scale)
                    p_ref[rs, c0:c0 + LANES] = e.astype(BF16)
                part = _dot(p_ref[rs, k0:k1], va_ref[k0:k1, :])
                o = part if o is None else o + part
            o_ref[:, g * ATT_HD:(g + 1) * ATT_HD] = (o[:, :ATT_HD] / o[:, ATT_HD:]).astype(BF16)

    @pl.when(qi < n_q)
    def _():
        attend(0)

    @pl.when(qi >= n_q)
    def _():
        attend(seq)


def _att_call(batch, seq, ctx_len, proj, ctx_out):
    tq = ATT_Q_TILE
    assert ctx_len == tq and seq % tq == 0
    n_q = seq // tq
    ctx_row0 = batch * seq // ctx_len
    group_w = (ATT_HEADS // ATT_KV_HEADS) * ATT_HD
    k_col = ATT_HEADS
    v_col = ATT_HEADS + ATT_KV_HEADS
    n_rows = batch * seq + (batch * ctx_len if ctx_out else 0)

    def q_map(b, kh, qi):
        return (jnp.where(qi < n_q, b * n_q + qi, ctx_row0 + b), kh)

    return pl.pallas_call(
        functools.partial(_att_kernel, n_q=n_q, seq=seq),
        grid=(batch, ATT_KV_HEADS, n_q + (1 if ctx_out else 0)),
        scratch_shapes=[pltpu.VMEM((group_w // ATT_HD * tq, seq + ctx_len), F32),
                        pltpu.VMEM((group_w // ATT_HD * tq, seq + ctx_len), BF16),
                        pltpu.VMEM((seq + ctx_len, 2 * ATT_HD), BF16)],
        in_specs=[
            pl.BlockSpec((tq, group_w), q_map),
            pl.BlockSpec((seq, ATT_HD), lambda b, kh, qi: (b, k_col + kh)),
            pl.BlockSpec((seq, ATT_HD), lambda b, kh, qi: (b, v_col + kh)),
            pl.BlockSpec((ctx_len, ATT_HD), lambda b, kh, qi: (ctx_row0 + b, k_col + kh)),
            pl.BlockSpec((ctx_len, ATT_HD), lambda b, kh, qi: (ctx_row0 + b, v_col + kh)),
        ],
        out_specs=pl.BlockSpec((tq, group_w), q_map),
        out_shape=jax.ShapeDtypeStruct((n_rows, ATT_HEADS * ATT_HD), BF16),
        compiler_params=_params(3),
        name="att",
    )(proj, proj, proj, proj, proj)


def _oproj_kernel(x_ref, mod_ref, w_ref, *rest, n_lat):
    gate = mod_ref[0, 0, 5:6, :]
    if len(rest) == 2:
        a_ref, o_ref = rest
        o_ref[...] = x_ref[...] + gate * _dot(a_ref[...], w_ref[...])
        return
    al_ref, ac_ref, o_ref = rest
    i = pl.program_id(0)

    @pl.when(i < n_lat)
    def _():
        o_ref[...] = x_ref[...] + gate * _dot(al_ref[...], w_ref[...])

    @pl.when(i >= n_lat)
    def _():
        o_ref[...] = x_ref[...] + gate * _dot(ac_ref[...], w_ref[...])


def _oproj_call(rows, n_tiles, x, mod, layer, w, which, a_lat, a_ctx=None):
    d = x.shape[1]
    k = w.shape[-2]
    n_lat = rows.n_lat
    in_specs = [rows.row_spec(d), rows.mod_spec(layer, d), _resident(w, (which,))]
    args = [x, mod, w]
    if a_ctx is None:
        in_specs.append(rows.row_spec(k))
        args.append(a_lat)
    else:
        in_specs.append(pl.BlockSpec((ROW_TILE, k), lambda i: (jnp.minimum(i, n_lat - 1), 0)))
        in_specs.append(pl.BlockSpec((ROW_TILE, k), lambda i: (jnp.maximum(i - n_lat, 0), 0)))
        args += [a_lat, a_ctx]
    return pl.pallas_call(
        functools.partial(_oproj_kernel, n_lat=n_lat),
        grid=(n_tiles,),
        in_specs=in_specs,
        out_specs=rows.row_spec(d),
        out_shape=jax.ShapeDtypeStruct((n_tiles * ROW_TILE, d), F32),
        compiler_params=_params(1),
        name="oproj",
    )(*args)


def _with_identity_rows(cos, sin, pad_rows):
    pad = (pad_rows, cos.shape[1])
    return (jnp.concatenate([cos, jnp.ones(pad, F32)], axis=0),
            jnp.concatenate([sin, jnp.zeros(pad, F32)], axis=0))


def _seq_rope_tables(seq, pad_rows):
    half = RET_DK // 2
    freqs = ROPE_THETA ** (-jnp.arange(half, dtype=F32) / half)
    ang = jnp.arange(seq, dtype=F32)[:, None] * freqs
    return _with_identity_rows(jnp.cos(ang), jnp.sin(ang), pad_rows)


def _axial_rope_tables(seq, pad_rows):
    quarter = ATT_HD // 4
    tok = jnp.arange(seq)
    r = (tok // GRID_W).astype(F32)
    cl = (tok % GRID_W).astype(F32)
    freqs = ROPE_THETA ** (-jnp.arange(quarter, dtype=F32) / quarter)
    ang = jnp.concatenate([r[:, None] * freqs, cl[:, None] * freqs], axis=-1)
    cos, sin = jnp.cos(ang), jnp.sin(ang)
    return _with_identity_rows(jnp.concatenate([cos, cos], axis=-1),
                               jnp.concatenate([-sin, sin], axis=-1), pad_rows)


def kernel(x, c, ctx, c_ctx, ada_w, ada_b, norm_g, ffn_w1, ffn_w2, ret_w_in, ret_w_out,
           ret_decay_f, ret_decay_b, att_w_qkv, att_w_o, att_q_gain, att_k_gain, final_g):
    batch, seq, d = x.shape
    ctx_len = ctx.shape[1]
    depth = ada_w.shape[0]
    rows = _Rows(batch, seq, ctx_len)
    rows_ffn = _Rows.widest(batch, seq, ctx_len, FFN_SUBTILES)

    cc = jnp.concatenate(
        [c, c_ctx[None], jnp.zeros((ADA_ROWS - batch - 1, d), F32)], axis=0)
    mod = _ada_call(cc, ada_w, ada_b).reshape(depth, ADA_ROWS, N_MOD, d)

    w1 = ffn_w1.astype(BF16)
    w2 = ffn_w2.astype(BF16)
    w_in = ret_w_in.astype(BF16)
    w_out = ret_w_out.astype(BF16)
    w_qkv = att_w_qkv.astype(BF16)
    w_o = att_w_o.astype(BF16)

    rows_att = _Rows.widest(batch, seq, ctx_len, PROJ_ATT_SUBTILES)
    ret_cos, ret_sin = _seq_rope_tables(seq, rows.tile)
    att_cos, att_sin = _axial_rope_tables(seq, rows_att.tile)
    ret_rope_specs = [rows.rope_spec(RET_DK // 2)] * 2
    att_rope_specs = [rows_att.rope_spec(ATT_HD)] * 2
    gain_spec = pl.BlockSpec((1, ATT_HD), lambda i: (0, 0))

    xs = x.reshape(batch * seq, d)
    xs_ctx = ctx.reshape(batch * ctx_len, d)

    for i in range(depth):
        last = i == depth - 1
        j = i // 2
        xs = _ffn_call(rows_ffn, False, xs, mod, i, 0, norm_g[i, 0], w1, w2, 0,
                       x_ctx=xs_ctx if i == 0 else None)
        n_tiles = rows.n_lat if last else rows.n_all
        if i % 2 == 0:
            proj = _proj_call(_proj_ret_kernel, "proj_ret", rows, xs, mod, i, norm_g[i, 1],
                              w_in, j, [ret_cos, ret_sin], ret_rope_specs)
            dec_tab = jnp.broadcast_to(
                jnp.stack([ret_decay_f[j], ret_decay_b[j]], axis=1)[:, :, None],
                (RET_HEADS, 2, 128))
            dec_tab = jnp.concatenate([dec_tab, jnp.zeros((RET_HEADS, 6, 128), F32)], axis=1)
            a_lat, a_ctx = _ret_call(batch, seq, ctx_len, proj, dec_tab)
            xs = _oproj_call(rows, n_tiles, xs, mod, i, w_out, j, a_lat,
                             None if last else a_ctx)
        else:
            proj = _proj_call(_proj_att_kernel, "proj_att", rows_att, xs, mod, i, norm_g[i, 1],
                              w_qkv, j,
                              [att_q_gain[j].reshape(1, ATT_HD), att_k_gain[j].reshape(1, ATT_HD),
                               att_cos, att_sin],
                              [gain_spec, gain_spec] + att_rope_specs)
            a = _att_call(batch, seq, ctx_len, proj, not last)
            xs = _oproj_call(rows, n_tiles, xs, mod, i, w_o, j, a)
        xs = _ffn_call(rows_ffn, last, xs, mod, i, 6, norm_g[i, 2], w1, w2, 1,
                       final_g if last else None)
    return xs.reshape(batch, seq, d)
```

```python
import functools

import jax
import jax.numpy as jnp
from jax import lax
from jax.experimental import pallas as pl
from jax.experimental.pallas import tpu as pltpu

F32 = jnp.float32
BF16 = jnp.bfloat16

N_MOD = 9
RET_HEADS = 4
RET_DK = 256
RET_DV = 512
ATT_HEADS = 8
ATT_KV_HEADS = 2
ATT_HD = 128
GRID_W = 64
ROPE_THETA = 10000.0
EPS = 1e-6

ROW_TILE = 512
RET_CHUNK = 256
ATT_Q_TILE = 256
ATT_KEY_CHUNK = 512
ATT_EXP2_SCALE = ATT_HD ** -0.5 * 1.4426950408889634
LANES = 128
MXU_DEPTH = 256
FFN_SPLIT = 2
FFN_SUBTILES = 2
PROJ_ATT_SUBTILES = 2
ADA_ROWS = 16
VMEM_LIMIT = 56 * 1024 * 1024


def _params(n_axes):
    return pltpu.CompilerParams(
        dimension_semantics=("arbitrary",) * n_axes, vmem_limit_bytes=VMEM_LIMIT)


def _resident(stacked, lead):
    tail = stacked.shape[len(lead):]
    index = tuple(lead) + (0,) * len(tail)
    return pl.BlockSpec((None,) * len(lead) + tail, lambda *_: index,
                        pipeline_mode=pl.Buffered(1))


def _silu(x):
    return x * jax.nn.sigmoid(x)


def _rms_rows(x):
    return x * lax.rsqrt(jnp.mean(x * x, axis=-1, keepdims=True) + EPS)


def _modulated(x, gain, shift, scale):
    return (_rms_rows(x) * gain) * (1.0 + scale) + shift


def _dot(a, b):
    return jnp.dot(a, b, preferred_element_type=F32)


def _dot_nt(a, b):
    return lax.dot_general(a, b, (((1,), (1,)), ((), ())), preferred_element_type=F32)


def _dot_tn(a, b):
    return lax.dot_general(a, b, (((0,), (0,)), ((), ())), preferred_element_type=F32)


def _ada_kernel(cc_ref, w_ref, b_ref, o_ref):
    s = _silu(cc_ref[...])
    o_ref[0] = _dot(s.astype(BF16), w_ref[0].astype(BF16)) + b_ref[0]


def _ada_call(cc, ada_w, ada_b):
    depth, d, n = ada_w.shape
    tn = 1024
    return pl.pallas_call(
        _ada_kernel,
        grid=(depth, n // tn),
        in_specs=[
            pl.BlockSpec((ADA_ROWS, d), lambda l, j: (0, 0)),
            pl.BlockSpec((1, d, tn), lambda l, j: (l, 0, j)),
            pl.BlockSpec((1, 1, tn), lambda l, j: (l, 0, j)),
        ],
        out_specs=pl.BlockSpec((1, ADA_ROWS, tn), lambda l, j: (l, 0, j)),
        out_shape=jax.ShapeDtypeStruct((depth, ADA_ROWS, n), F32),
        compiler_params=_params(2),
        name="ada",
    )(cc, ada_w, ada_b.reshape(depth, 1, n))


class _Rows:
    def __init__(self, batch, seq, ctx_len, tile=ROW_TILE):
        self.batch, self.seq, self.ctx_len, self.tile = batch, seq, ctx_len, tile
        assert seq % tile == 0 and (batch * ctx_len) % tile == 0
        self.tiles_per_batch = seq // tile
        self.n_lat = batch * seq // tile
        self.n_ctx = batch * ctx_len // tile
        self.n_all = self.n_lat + self.n_ctx

    @classmethod
    def widest(cls, batch, seq, ctx_len, factor):
        tile = ROW_TILE * factor
        if seq % tile or (batch * ctx_len) % tile:
            tile = ROW_TILE
        return cls(batch, seq, ctx_len, tile)

    def mod_spec(self, layer, d):
        n_lat, tpb, batch = self.n_lat, self.tiles_per_batch, self.batch
        return pl.BlockSpec(
            (1, 1, N_MOD, d),
            lambda i: (layer, jnp.where(i < n_lat, i // tpb, batch), 0, 0))

    def rope_spec(self, width):
        n_lat, tpb = self.n_lat, self.tiles_per_batch
        return pl.BlockSpec((self.tile, width), lambda i: (jnp.where(i < n_lat, i % tpb, tpb), 0))

    def row_spec(self, width):
        return pl.BlockSpec((self.tile, width), lambda i: (i, 0))


def _mod_rows(mod_ref, base):
    return [mod_ref[0, 0, base + k:base + k + 1, :] for k in range(3)]


def _ffn_kernel(*refs, base, ffn_dim, chunks, final, n_lat_split, mixer):
    refs = list(refs)
    o_ref = refs.pop()
    fg_ref = refs.pop() if final else None
    x_ref = refs.pop(0)
    xc_ref = refs.pop(0) if n_lat_split is not None else None
    a_ref, wm_ref = (refs.pop(0), refs.pop(0)) if mixer else (None, None)
    mod_ref, g_ref, w1_ref, w2_ref = refs
    shift, scale, gate = _mod_rows(mod_ref, base)
    for r0 in range(0, o_ref.shape[0], ROW_TILE):
        rs = slice(r0, r0 + ROW_TILE)
        x = x_ref[rs, :]
        if xc_ref is not None:
            x = jnp.where(pl.program_id(0) < n_lat_split, x, xc_ref[rs, :])
        if mixer:
            x = x + mod_ref[0, 0, 5:6, :] * _dot(a_ref[rs, :], wm_ref[...])
        h = _modulated(x, g_ref[...], shift, scale).astype(BF16)
        acc = None
        lo = 0
        for width in chunks:
            gt = _dot(h, w1_ref[:, lo:lo + width])
            up = _dot(h, w1_ref[:, ffn_dim + lo:ffn_dim + lo + width])
            a = (_silu(gt) * up).astype(BF16)
            part = _dot(a, w2_ref[lo:lo + width, :])
            acc = part if acc is None else acc + part
            lo += width
        y = x + (0.5 * gate) * acc
        if final:
            y = _rms_rows(y) * fg_ref[...]
        o_ref[rs, :] = y


def _ffn_chunks(ffn_dim):
    assert ffn_dim % MXU_DEPTH == 0
    n_tiles = ffn_dim // MXU_DEPTH
    first = (n_tiles // FFN_SPLIT) * MXU_DEPTH
    sizes = [first] * (FFN_SPLIT - 1) + [ffn_dim - first * (FFN_SPLIT - 1)]
    return tuple(s for s in sizes if s)


def _ffn_call(rows, latent_only, x, mod, layer, base, gain, w1, w2, which, final_g=None,
              x_ctx=None, mixer=None):
    d = x.shape[1]
    ffn_dim = w2.shape[-2]
    chunks = _ffn_chunks(ffn_dim)
    final = final_g is not None
    n_lat = rows.n_lat
    n_tiles = n_lat if latent_only else rows.n_all
    if x_ctx is None:
        in_specs, args = [rows.row_spec(d)], [x]
    else:
        in_specs = [pl.BlockSpec((rows.tile, d), lambda i: (jnp.minimum(i, n_lat - 1), 0)),
                    pl.BlockSpec((rows.tile, d), lambda i: (jnp.maximum(i - n_lat, 0), 0))]
        args = [x, x_ctx]
    if mixer is not None:
        a, w_mix, mix_idx = mixer
        in_specs += [rows.row_spec(a.shape[1]), _resident(w_mix, (mix_idx,))]
        args += [a, w_mix]
    in_specs += [rows.mod_spec(layer, d), pl.BlockSpec((1, d), lambda i: (0, 0)),
                 _resident(w1, (layer, which)), _resident(w2, (layer, which))]
    args += [mod, gain.reshape(1, d), w1, w2]
    if final:
        in_specs.append(pl.BlockSpec((1, d), lambda i: (0, 0)))
        args.append(final_g.reshape(1, d))
    return pl.pallas_call(
        functools.partial(_ffn_kernel, base=base, ffn_dim=ffn_dim, chunks=chunks, final=final,
                          n_lat_split=None if x_ctx is None else n_lat,
                          mixer=mixer is not None),
        grid=(n_tiles,),
        in_specs=in_specs,
        out_specs=rows.row_spec(d),
        out_shape=jax.ShapeDtypeStruct((n_tiles * rows.tile, d), F32),
        compiler_params=_params(1),
        name="ffn",
    )(*args)


def _rope_halves(x1, x2, cos, sin):
    return x1 * cos - x2 * sin, x2 * cos + x1 * sin


def _proj_ret_kernel(x_ref, mod_ref, g_ref, w_ref, cos_ref, sin_ref, o_ref):
    shift, scale, _ = _mod_rows(mod_ref, 3)
    h = _modulated(x_ref[...], g_ref[...], shift, scale).astype(BF16)
    cos, sin = cos_ref[...], sin_ref[...]
    qd = RET_HEADS * RET_DK
    vd = RET_HEADS * RET_DV
    half = RET_DK // 2
    for part, mult in ((0, None), (1, RET_DK ** -0.5)):
        p = _dot(h, w_ref[:, part * qd:(part + 1) * qd])
        if mult is not None:
            p = p * mult
        for hh in range(RET_HEADS):
            lo = hh * RET_DK
            r1, r2 = _rope_halves(p[:, lo:lo + half], p[:, lo + half:lo + RET_DK], cos, sin)
            o_ref[:, part * qd + lo:part * qd + lo + half] = r1.astype(BF16)
            o_ref[:, part * qd + lo + half:part * qd + lo + RET_DK] = r2.astype(BF16)
    for part in range(2):
        lo = 2 * qd + part * vd
        o_ref[:, lo:lo + vd] = _dot(h, w_ref[:, lo:lo + vd]).astype(BF16)


def _proj_att_kernel(x_ref, mod_ref, g_ref, w_ref, qg_ref, kg_ref, cos_ref, sin_ref, o_ref):
    shift, scale, _ = _mod_rows(mod_ref, 3)
    qd = ATT_HEADS * ATT_HD
    kd = ATT_KV_HEADS * ATT_HD
    for r0 in range(0, o_ref.shape[0], ROW_TILE):
        rs = slice(r0, r0 + ROW_TILE)
        h = _modulated(x_ref[rs, :], g_ref[...], shift, scale).astype(BF16)
        cos, sin = cos_ref[rs, :], sin_ref[rs, :]

        def norm_rope(p, n_heads, gain, out_lo):
            for hh in range(n_heads):
                lo = hh * ATT_HD
                y = _rms_rows(p[:, lo:lo + ATT_HD]) * gain
                r = y * cos + pltpu.roll(y, ATT_HD // 2, 1) * sin
                o_ref[rs, out_lo + lo:out_lo + lo + ATT_HD] = r.astype(BF16)

        norm_rope(_dot(h, w_ref[:, :qd]), ATT_HEADS, qg_ref[...] * ATT_EXP2_SCALE, 0)
        norm_rope(_dot(h, w_ref[:, qd:qd + kd]), ATT_KV_HEADS, kg_ref[...], qd)
        o_ref[rs, qd + kd:] = _dot(h, w_ref[:, qd + kd:]).astype(BF16)


def _proj_call(kernel, name, rows, x, mod, layer, gain, w, which, extras, extra_specs):
    d = x.shape[1]
    n_out = w.shape[-1]
    return pl.pallas_call(
        kernel,
        grid=(rows.n_all,),
        in_specs=[rows.row_spec(d), rows.mod_spec(layer, d), pl.BlockSpec((1, d), lambda i: (0, 0)),
                  _resident(w, (which,))] + extra_specs,
        out_specs=rows.row_spec(n_out),
        out_shape=jax.ShapeDtypeStruct((rows.n_all * rows.tile, n_out), BF16),
        compiler_params=_params(1),
        name=name,
    )(x, mod, gain.reshape(1, d), w, *extras)


def _decay_tables(n, lg_f, lg_b):
    i = lax.broadcasted_iota(jnp.int32, (n, n), 0)
    j = lax.broadcasted_iota(jnp.int32, (n, n), 1)
    diff = (i - j).astype(F32)
    mask = (jnp.where(diff >= 0, jnp.exp(jnp.maximum(diff, 0.0) * lg_f), 0.0)
            + jnp.where(diff <= 0, jnp.exp(jnp.maximum(-diff, 0.0) * lg_b), 0.0))
    pos = lax.broadcasted_iota(jnp.int32, (n, 1), 0).astype(F32)
    xi_f = jnp.exp((pos + 1.0) * lg_f)
    xi_b = jnp.exp((n - pos) * lg_b)
    zeta_f = jnp.exp((n - 1.0 - pos) * lg_f)
    zeta_b = jnp.exp(pos * lg_b)
    return mask, xi_f, xi_b, zeta_f, zeta_b


def _scaled_bf16(x, col):
    return (x.astype(F32) * col).astype(BF16)


def _head_norm_gate(o, g):
    return (_rms_rows(o) * _silu(g.astype(F32))).astype(BF16)


def _scaled_pair(x, col_a, col_b):
    xf = x.astype(F32)
    return jnp.concatenate([(xf * col_a).astype(BF16), (xf * col_b).astype(BF16)], axis=1)


def _ret_kernel(ql_ref, kl_ref, vl_ref, gl_ref, qc_ref, kc_ref, vc_ref, gc_ref, dec_ref,
                ol_ref, oc_ref, st_ref, ub_ref, runf_ref, runb_ref, mask_ref, cmask_ref,
                col_ref, ccol_ref, *, n_chunks, chunk, ctx_len):
    dk = RET_DK
    lg_f = -jnp.exp(dec_ref[0, 0:1, 0:1])
    lg_b = -jnp.exp(dec_ref[0, 1:2, 0:1])

    @pl.when(pl.program_id(1) == 0)
    def _():
        mask, xi_f, xi_b, zeta_f, zeta_b = _decay_tables(chunk, lg_f, lg_b)
        mask_ref[...] = mask
        for k, col in enumerate((xi_f, xi_b, zeta_f, zeta_b)):
            col_ref[k] = col
        cmask, _, _, czeta_f, czeta_b = _decay_tables(ctx_len, lg_f, lg_b)
        cmask_ref[...] = cmask
        ccol_ref[0] = czeta_f
        ccol_ref[1] = czeta_b

    decay_f = jnp.exp(chunk * lg_f)
    decay_b = jnp.exp(chunk * lg_b)

    q, k, v = qc_ref[...], kc_ref[...], vc_ref[...]
    p = (_dot_nt(q, k) * cmask_ref[...]).astype(BF16)
    oc_ref[...] = _head_norm_gate(_dot(p, v), gc_ref[...])
    s0 = _dot_tn(_scaled_pair(k, ccol_ref[0], ccol_ref[1]), v)
    runf_ref[...] = s0[:dk]
    runb_ref[...] = s0[dk:]
    st_ref[0, :dk, :] = s0[:dk].astype(BF16)
    st_ref[n_chunks - 1, dk:, :] = s0[dk:].astype(BF16)

    def rows(n):
        return slice(n * chunk, (n + 1) * chunk)

    for n in range(n_chunks):
        u = _dot_tn(_scaled_pair(kl_ref[rows(n), :], col_ref[2], col_ref[3]), vl_ref[rows(n), :])
        if n + 1 < n_chunks:
            s = runf_ref[...] * decay_f + u[:dk]
            runf_ref[...] = s
            st_ref[n + 1, :dk, :] = s.astype(BF16)
        if n > 0:
            ub_ref[n] = u[dk:]

    for n in range(n_chunks - 1, 0, -1):
        s = runb_ref[...] * decay_b + ub_ref[n]
        runb_ref[...] = s
        st_ref[n - 1, dk:, :] = s.astype(BF16)

    for n in range(n_chunks):
        r = rows(n)
        q, k, v = ql_ref[r, :], kl_ref[r, :], vl_ref[r, :]
        p = (_dot_nt(q, k) * mask_ref[...]).astype(BF16)
        o = _dot(p, v) + _dot(_scaled_pair(q, col_ref[0], col_ref[1]), st_ref[n])
        ol_ref[r, :] = _head_norm_gate(o, gl_ref[r, :])


def _ret_call(batch, seq, ctx_len, proj, dec_tab):
    chunk = min(RET_CHUNK, seq)
    assert seq % chunk == 0
    n_chunks = seq // chunk
    ctx_row0 = batch * seq // ctx_len
    kq = RET_HEADS
    kv = 2 * RET_HEADS * RET_DK // RET_DV
    kg = kv + RET_HEADS
    vd = RET_HEADS * RET_DV
    in_specs = [
        pl.BlockSpec((seq, RET_DK), lambda h, b: (b, h)),
        pl.BlockSpec((seq, RET_DK), lambda h, b: (b, kq + h)),
        pl.BlockSpec((seq, RET_DV), lambda h, b: (b, kv + h)),
        pl.BlockSpec((seq, RET_DV), lambda h, b: (b, kg + h)),
        pl.BlockSpec((ctx_len, RET_DK), lambda h, b: (ctx_row0 + b, h)),
        pl.BlockSpec((ctx_len, RET_DK), lambda h, b: (ctx_row0 + b, kq + h)),
        pl.BlockSpec((ctx_len, RET_DV), lambda h, b: (ctx_row0 + b, kv + h)),
        pl.BlockSpec((ctx_len, RET_DV), lambda h, b: (ctx_row0 + b, kg + h)),
        pl.BlockSpec((1, 8, 128), lambda h, b: (h, 0, 0)),
    ]
    return pl.pallas_call(
        functools.partial(_ret_kernel, n_chunks=n_chunks, chunk=chunk, ctx_len=ctx_len),
        grid=(RET_HEADS, batch),
        in_specs=in_specs,
        out_specs=[pl.BlockSpec((seq, RET_DV), lambda h, b: (b, h)),
                   pl.BlockSpec((ctx_len, RET_DV), lambda h, b: (b, h))],
        out_shape=[jax.ShapeDtypeStruct((batch * seq, vd), BF16),
                   jax.ShapeDtypeStruct((batch * ctx_len, vd), BF16)],
        scratch_shapes=[pltpu.VMEM((n_chunks, 2 * RET_DK, RET_DV), BF16),
                        pltpu.VMEM((n_chunks, RET_DK, RET_DV), F32),
                        pltpu.VMEM((RET_DK, RET_DV), F32),
                        pltpu.VMEM((RET_DK, RET_DV), F32),
                        pltpu.VMEM((chunk, chunk), F32),
                        pltpu.VMEM((ctx_len, ctx_len), F32),
                        pltpu.VMEM((4, chunk, 1), F32),
                        pltpu.VMEM((2, ctx_len, 1), F32)],
        compiler_params=_params(2),
        name="ret",
    )(proj, proj, proj, proj, proj, proj, proj, proj, dec_tab)


def _att_kernel(q_ref, kl_ref, vl_ref, kc_ref, vc_ref, o_ref, s_ref, p_ref, va_ref, *, n_q, seq):
    qi = pl.program_id(2)
    group = ATT_HEADS // ATT_KV_HEADS
    tq = q_ref.shape[0]
    n_keys = s_ref.shape[1]

    @pl.when(qi == 0)
    def _():
        va_ref[:seq, :ATT_HD] = vl_ref[...]
        va_ref[seq:, :ATT_HD] = vc_ref[...]
        va_ref[:, ATT_HD:] = jnp.ones((n_keys, ATT_HD), BF16)

    def attend(col_lo):
        slabs = range(col_lo, n_keys, LANES)
        for g in range(group):
            q = q_ref[:, g * ATT_HD:(g + 1) * ATT_HD]
            if col_lo < seq:
                s_ref[g * tq:(g + 1) * tq, :seq] = _dot_nt(q, kl_ref[...])
            s_ref[g * tq:(g + 1) * tq, seq:] = _dot_nt(q, kc_ref[...])
        for g in range(group):
            rs = slice(g * tq, (g + 1) * tq)
            m = None
            for c0 in slabs:
                t = s_ref[rs, c0:c0 + LANES]
                m = t if m is None else jnp.maximum(m, t)
            m = jnp.max(m, axis=-1, keepdims=True)
            o = None
            for k0 in range(col_lo, n_keys, ATT_KEY_CHUNK):
                k1 = min(k0 + ATT_KEY_CHUNK, seq if k0 < seq else n_keys)
                for c0 in range(k0, k1, LANES):
                    e = jnp.exp2(s_ref[rs, c0:c0 + LANES] - m)
                    p_ref[rs, c0:c0 + LANES] = e.astype(BF16)
                part = _dot(p_ref[rs, k0:k1], va_ref[k0:k1, :])
                o = part if o is None else o + part
            o_ref[:, g * ATT_HD:(g + 1) * ATT_HD] = (o[:, :ATT_HD] / o[:, ATT_HD:]).astype(BF16)

    @pl.when(qi < n_q)
    def _():
        attend(0)

    @pl.when(qi >= n_q)
    def _():
        attend(seq)


def _att_call(batch, seq, ctx_len, proj, ctx_out):
    tq = ATT_Q_TILE
    assert ctx_len == tq and seq % tq == 0
    n_q = seq // tq
    ctx_row0 = batch * seq // ctx_len
    group_w = (ATT_HEADS // ATT_KV_HEADS) * ATT_HD
    k_col = ATT_HEADS
    v_col = ATT_HEADS + ATT_KV_HEADS
    n_rows = batch * seq + (batch * ctx_len if ctx_out else 0)

    def q_map(b, kh, qi):
        return (jnp.where(qi < n_q, b * n_q + qi, ctx_row0 + b), kh)

    return pl.pallas_call(
        functools.partial(_att_kernel, n_q=n_q, seq=seq),
        grid=(batch, ATT_KV_HEADS, n_q + (1 if ctx_out else 0)),
        scratch_shapes=[pltpu.VMEM((group_w // ATT_HD * tq, seq + ctx_len), F32),
                        pltpu.VMEM((group_w // ATT_HD * tq, seq + ctx_len), BF16),
                        pltpu.VMEM((seq + ctx_len, 2 * ATT_HD), BF16)],
        in_specs=[
            pl.BlockSpec((tq, group_w), q_map),
            pl.BlockSpec((seq, ATT_HD), lambda b, kh, qi: (b, k_col + kh)),
            pl.BlockSpec((seq, ATT_HD), lambda b, kh, qi: (b, v_col + kh)),
            pl.BlockSpec((ctx_len, ATT_HD), lambda b, kh, qi: (ctx_row0 + b, k_col + kh)),
            pl.BlockSpec((ctx_len, ATT_HD), lambda b, kh, qi: (ctx_row0 + b, v_col + kh)),
        ],
        out_specs=pl.BlockSpec((tq, group_w), q_map),
        out_shape=jax.ShapeDtypeStruct((n_rows, ATT_HEADS * ATT_HD), BF16),
        compiler_params=_params(3),
        name="att",
    )(proj, proj, proj, proj, proj)


def _oproj_kernel(x_ref, mod_ref, w_ref, *rest, n_lat):
    gate = mod_ref[0, 0, 5:6, :]
    if len(rest) == 2:
        a_ref, o_ref = rest
        o_ref[...] = x_ref[...] + gate * _dot(a_ref[...], w_ref[...])
        return
    al_ref, ac_ref, o_ref = rest
    i = pl.program_id(0)

    @pl.when(i < n_lat)
    def _():
        o_ref[...] = x_ref[...] + gate * _dot(al_ref[...], w_ref[...])

    @pl.when(i >= n_lat)
    def _():
        o_ref[...] = x_ref[...] + gate * _dot(ac_ref[...], w_ref[...])


def _oproj_call(rows, n_tiles, x, mod, layer, w, which, a_lat, a_ctx=None):
    d = x.shape[1]
    k = w.shape[-2]
    n_lat = rows.n_lat
    in_specs = [rows.row_spec(d), rows.mod_spec(layer, d), _resident(w, (which,))]
    args = [x, mod, w]
    if a_ctx is None:
        in_specs.append(rows.row_spec(k))
        args.append(a_lat)
    else:
        in_specs.append(pl.BlockSpec((ROW_TILE, k), lambda i: (jnp.minimum(i, n_lat - 1), 0)))
        in_specs.append(pl.BlockSpec((ROW_TILE, k), lambda i: (jnp.maximum(i - n_lat, 0), 0)))
        args += [a_lat, a_ctx]
    return pl.pallas_call(
        functools.partial(_oproj_kernel, n_lat=n_lat),
        grid=(n_tiles,),
        in_specs=in_specs,
        out_specs=rows.row_spec(d),
        out_shape=jax.ShapeDtypeStruct((n_tiles * ROW_TILE, d), F32),
        compiler_params=_params(1),
        name="oproj",
    )(*args)


def _with_identity_rows(cos, sin, pad_rows):
    pad = (pad_rows, cos.shape[1])
    return (jnp.concatenate([cos, jnp.ones(pad, F32)], axis=0),
            jnp.concatenate([sin, jnp.zeros(pad, F32)], axis=0))


def _seq_rope_tables(seq, pad_rows):
    half = RET_DK // 2
    freqs = ROPE_THETA ** (-jnp.arange(half, dtype=F32) / half)
    ang = jnp.arange(seq, dtype=F32)[:, None] * freqs
    return _with_identity_rows(jnp.cos(ang), jnp.sin(ang), pad_rows)


def _axial_rope_tables(seq, pad_rows):
    quarter = ATT_HD // 4
    tok = jnp.arange(seq)
    r = (tok // GRID_W).astype(F32)
    cl = (tok % GRID_W).astype(F32)
    freqs = ROPE_THETA ** (-jnp.arange(quarter, dtype=F32) / quarter)
    ang = jnp.concatenate([r[:, None] * freqs, cl[:, None] * freqs], axis=-1)
    cos, sin = jnp.cos(ang), jnp.sin(ang)
    return _with_identity_rows(jnp.concatenate([cos, cos], axis=-1),
                               jnp.concatenate([-sin, sin], axis=-1), pad_rows)


def kernel(x, c, ctx, c_ctx, ada_w, ada_b, norm_g, ffn_w1, ffn_w2, ret_w_in, ret_w_out,
           ret_decay_f, ret_decay_b, att_w_qkv, att_w_o, att_q_gain, att_k_gain, final_g):
    batch, seq, d = x.shape
    ctx_len = ctx.shape[1]
    depth = ada_w.shape[0]
    rows = _Rows(batch, seq, ctx_len)
    rows_ffn = _Rows.widest(batch, seq, ctx_len, FFN_SUBTILES)

    cc = jnp.concatenate(
        [c, c_ctx[None], jnp.zeros((ADA_ROWS - batch - 1, d), F32)], axis=0)
    mod = _ada_call(cc, ada_w, ada_b).reshape(depth, ADA_ROWS, N_MOD, d)

    w1 = ffn_w1.astype(BF16)
    w2 = ffn_w2.astype(BF16)
    w_in = ret_w_in.astype(BF16)
    w_out = ret_w_out.astype(BF16)
    w_qkv = att_w_qkv.astype(BF16)
    w_o = att_w_o.astype(BF16)

    rows_att = _Rows.widest(batch, seq, ctx_len, PROJ_ATT_SUBTILES)
    ret_cos, ret_sin = _seq_rope_tables(seq, rows.tile)
    att_cos, att_sin = _axial_rope_tables(seq, rows_att.tile)
    ret_rope_specs = [rows.rope_spec(RET_DK // 2)] * 2
    att_rope_specs = [rows_att.rope_spec(ATT_HD)] * 2
    gain_spec = pl.BlockSpec((1, ATT_HD), lambda i: (0, 0))

    xs = x.reshape(batch * seq, d)
    xs_ctx = ctx.reshape(batch * ctx_len, d)

    for i in range(depth):
        last = i == depth - 1
        j = i // 2
        xs = _ffn_call(rows_ffn, False, xs, mod, i, 0, norm_g[i, 0], w1, w2, 0,
                       x_ctx=xs_ctx if i == 0 else None)
        n_tiles = rows.n_lat if last else rows.n_all
        if i % 2 == 0:
            proj = _proj_call(_proj_ret_kernel, "proj_ret", rows, xs, mod, i, norm_g[i, 1],
                              w_in, j, [ret_cos, ret_sin], ret_rope_specs)
            dec_tab = jnp.broadcast_to(
                jnp.stack([ret_decay_f[j], ret_decay_b[j]], axis=1)[:, :, None],
                (RET_HEADS, 2, 128))
            dec_tab = jnp.concatenate([dec_tab, jnp.zeros((RET_HEADS, 6, 128), F32)], axis=1)
            a_lat, a_ctx = _ret_call(batch, seq, ctx_len, proj, dec_tab)
            xs = _oproj_call(rows, n_tiles, xs, mod, i, w_out, j, a_lat,
                             None if last else a_ctx)
            mixer = None
        else:
            proj = _proj_call(_proj_att_kernel, "proj_att", rows_att, xs, mod, i, norm_g[i, 1],
                              w_qkv, j,
                              [att_q_gain[j].reshape(1, ATT_HD), att_k_gain[j].reshape(1, ATT_HD),
                               att_cos, att_sin],
                              [gain_spec, gain_spec] + att_rope_specs)
            mixer = (_att_call(batch, seq, ctx_len, proj, not last), w_o, j)
        xs = _ffn_call(rows_ffn, last, xs, mod, i, 6, norm_g[i, 2], w1, w2, 1,
                       final_g if last else None, mixer=mixer)
    return xs.reshape(batch, seq, d)
```

```python
import functools

import jax
import jax.numpy as jnp
from jax import lax
from jax.experimental import pallas as pl
from jax.experimental.pallas import tpu as pltpu

F32 = jnp.float32
BF16 = jnp.bfloat16

N_MOD = 9
RET_HEADS = 4
RET_DK = 256
RET_DV = 512
ATT_HEADS = 8
ATT_KV_HEADS = 2
ATT_HD = 128
GRID_W = 64
ROPE_THETA = 10000.0
EPS = 1e-6

ROW_TILE = 512
RET_CHUNK = 256
ATT_Q_TILE = 256
ATT_KEY_CHUNK = 512
ATT_EXP2_SCALE = ATT_HD ** -0.5 * 1.4426950408889634
LANES = 128
MXU_DEPTH = 256
FFN_SPLIT = 1
FFN_SUBTILES = 2
PROJ_ATT_SUBTILES = 2
ADA_ROWS = 16
VMEM_LIMIT = 56 * 1024 * 1024


def _params(n_axes):
    return pltpu.CompilerParams(
        dimension_semantics=("arbitrary",) * n_axes, vmem_limit_bytes=VMEM_LIMIT)


def _resident(stacked, lead):
    tail = stacked.shape[len(lead):]
    index = tuple(lead) + (0,) * len(tail)
    return pl.BlockSpec((None,) * len(lead) + tail, lambda *_: index,
                        pipeline_mode=pl.Buffered(1))


def _silu(x):
    return x * jax.nn.sigmoid(x)


def _rms_rows(x):
    return x * lax.rsqrt(jnp.mean(x * x, axis=-1, keepdims=True) + EPS)


def _modulated(x, gain, shift, scale):
    return (_rms_rows(x) * gain) * (1.0 + scale) + shift


def _dot(a, b):
    return jnp.dot(a, b, preferred_element_type=F32)


def _dot_nt(a, b):
    return lax.dot_general(a, b, (((1,), (1,)), ((), ())), preferred_element_type=F32)


def _dot_tn(a, b):
    return lax.dot_general(a, b, (((0,), (0,)), ((), ())), preferred_element_type=F32)


def _ada_kernel(cc_ref, w_ref, b_ref, o_ref):
    s = _silu(cc_ref[...])
    o_ref[0] = _dot(s.astype(BF16), w_ref[0].astype(BF16)) + b_ref[0]


def _ada_call(cc, ada_w, ada_b):
    depth, d, n = ada_w.shape
    tn = 1024
    return pl.pallas_call(
        _ada_kernel,
        grid=(depth, n // tn),
        in_specs=[
            pl.BlockSpec((ADA_ROWS, d), lambda l, j: (0, 0)),
            pl.BlockSpec((1, d, tn), lambda l, j: (l, 0, j)),
            pl.BlockSpec((1, 1, tn), lambda l, j: (l, 0, j)),
        ],
        out_specs=pl.BlockSpec((1, ADA_ROWS, tn), lambda l, j: (l, 0, j)),
        out_shape=jax.ShapeDtypeStruct((depth, ADA_ROWS, n), F32),
        compiler_params=_params(2),
        name="ada",
    )(cc, ada_w, ada_b.reshape(depth, 1, n))


class _Rows:
    def __init__(self, batch, seq, ctx_len, tile=ROW_TILE):
        self.batch, self.seq, self.ctx_len, self.tile = batch, seq, ctx_len, tile
        assert seq % tile == 0 and (batch * ctx_len) % tile == 0
        self.tiles_per_batch = seq // tile
        self.n_lat = batch * seq // tile
        self.n_ctx = batch * ctx_len // tile
        self.n_all = self.n_lat + self.n_ctx

    @classmethod
    def widest(cls, batch, seq, ctx_len, factor):
        tile = ROW_TILE * factor
        if seq % tile or (batch * ctx_len) % tile:
            tile = ROW_TILE
        return cls(batch, seq, ctx_len, tile)

    def mod_spec(self, layer, d):
        n_lat, tpb, batch = self.n_lat, self.tiles_per_batch, self.batch
        return pl.BlockSpec(
            (1, 1, N_MOD, d),
            lambda i: (layer, jnp.where(i < n_lat, i // tpb, batch), 0, 0))

    def rope_spec(self, width):
        n_lat, tpb = self.n_lat, self.tiles_per_batch
        return pl.BlockSpec((self.tile, width), lambda i: (jnp.where(i < n_lat, i % tpb, tpb), 0))

    def row_spec(self, width):
        return pl.BlockSpec((self.tile, width), lambda i: (i, 0))


def _mod_rows(mod_ref, base):
    return [mod_ref[0, 0, base + k:base + k + 1, :] for k in range(3)]


def _ffn_kernel(*refs, base, ffn_dim, chunks, final, n_lat_split, mixer):
    refs = list(refs)
    o_ref = refs.pop()
    fg_ref = refs.pop() if final else None
    x_ref = refs.pop(0)
    xc_ref = refs.pop(0) if n_lat_split is not None else None
    a_ref, wm_ref = (refs.pop(0), refs.pop(0)) if mixer else (None, None)
    mod_ref, g_ref, w1_ref, w2_ref = refs
    shift, scale, gate = _mod_rows(mod_ref, base)
    for r0 in range(0, o_ref.shape[0], ROW_TILE):
        rs = slice(r0, r0 + ROW_TILE)
        x = x_ref[rs, :]
        if xc_ref is not None:
            x = jnp.where(pl.program_id(0) < n_lat_split, x, xc_ref[rs, :])
        if mixer:
            x = x + mod_ref[0, 0, 5:6, :] * _dot(a_ref[rs, :], wm_ref[...])
        h = _modulated(x, g_ref[...], shift, scale).astype(BF16)
        acc = None
        lo = 0
        for width in chunks:
            gt = _dot(h, w1_ref[:, lo:lo + width])
            up = _dot(h, w1_ref[:, ffn_dim + lo:ffn_dim + lo + width])
            a = (_silu(gt) * up).astype(BF16)
            part = _dot(a, w2_ref[lo:lo + width, :])
            acc = part if acc is None else acc + part
            lo += width
        y = x + (0.5 * gate) * acc
        if final:
            y = _rms_rows(y) * fg_ref[...]
        o_ref[rs, :] = y


def _ffn_chunks(ffn_dim):
    assert ffn_dim % MXU_DEPTH == 0
    n_tiles = ffn_dim // MXU_DEPTH
    first = (n_tiles // FFN_SPLIT) * MXU_DEPTH
    sizes = [first] * (FFN_SPLIT - 1) + [ffn_dim - first * (FFN_SPLIT - 1)]
    return tuple(s for s in sizes if s)


def _ffn_call(rows, latent_only, x, mod, layer, base, gain, w1, w2, which, final_g=None,
              x_ctx=None, mixer=None):
    d = x.shape[1]
    ffn_dim = w2.shape[-2]
    chunks = _ffn_chunks(ffn_dim)
    final = final_g is not None
    n_lat = rows.n_lat
    n_tiles = n_lat if latent_only else rows.n_all
    if x_ctx is None:
        in_specs, args = [rows.row_spec(d)], [x]
    else:
        in_specs = [pl.BlockSpec((rows.tile, d), lambda i: (jnp.minimum(i, n_lat - 1), 0)),
                    pl.BlockSpec((rows.tile, d), lambda i: (jnp.maximum(i - n_lat, 0), 0))]
        args = [x, x_ctx]
    if mixer is not None:
        a, w_mix, mix_idx = mixer
        in_specs += [rows.row_spec(a.shape[1]), _resident(w_mix, (mix_idx,))]
        args += [a, w_mix]
    in_specs += [rows.mod_spec(layer, d), pl.BlockSpec((1, d), lambda i: (0, 0)),
                 _resident(w1, (layer, which)), _resident(w2, (layer, which))]
    args += [mod, gain.reshape(1, d), w1, w2]
    if final:
        in_specs.append(pl.BlockSpec((1, d), lambda i: (0, 0)))
        args.append(final_g.reshape(1, d))
    return pl.pallas_call(
        functools.partial(_ffn_kernel, base=base, ffn_dim=ffn_dim, chunks=chunks, final=final,
                          n_lat_split=None if x_ctx is None else n_lat,
                          mixer=mixer is not None),
        grid=(n_tiles,),
        in_specs=in_specs,
        out_specs=rows.row_spec(d),
        out_shape=jax.ShapeDtypeStruct((n_tiles * rows.tile, d), F32),
        compiler_params=_params(1),
        name="ffn",
    )(*args)


def _rope_halves(x1, x2, cos, sin):
    return x1 * cos - x2 * sin, x2 * cos + x1 * sin


def _proj_ret_kernel(x_ref, mod_ref, g_ref, w_ref, cos_ref, sin_ref, o_ref):
    shift, scale, _ = _mod_rows(mod_ref, 3)
    h = _modulated(x_ref[...], g_ref[...], shift, scale).astype(BF16)
    cos, sin = cos_ref[...], sin_ref[...]
    qd = RET_HEADS * RET_DK
    vd = RET_HEADS * RET_DV
    half = RET_DK // 2
    for part, mult in ((0, None), (1, RET_DK ** -0.5)):
        p = _dot(h, w_ref[:, part * qd:(part + 1) * qd])
        if mult is not None:
            p = p * mult
        for hh in range(RET_HEADS):
            lo = hh * RET_DK
            r1, r2 = _rope_halves(p[:, lo:lo + half], p[:, lo + half:lo + RET_DK], cos, sin)
            o_ref[:, part * qd + lo:part * qd + lo + half] = r1.astype(BF16)
            o_ref[:, part * qd + lo + half:part * qd + lo + RET_DK] = r2.astype(BF16)
    for part in range(2):
        lo = 2 * qd + part * vd
        o_ref[:, lo:lo + vd] = _dot(h, w_ref[:, lo:lo + vd]).astype(BF16)


def _proj_att_kernel(x_ref, mod_ref, g_ref, w_ref, qg_ref, kg_ref, cos_ref, sin_ref, o_ref):
    shift, scale, _ = _mod_rows(mod_ref, 3)
    qd = ATT_HEADS * ATT_HD
    kd = ATT_KV_HEADS * ATT_HD
    for r0 in range(0, o_ref.shape[0], ROW_TILE):
        rs = slice(r0, r0 + ROW_TILE)
        h = _modulated(x_ref[rs, :], g_ref[...], shift, scale).astype(BF16)
        cos, sin = cos_ref[rs, :], sin_ref[rs, :]

        def norm_rope(p, n_heads, gain, out_lo):
            for hh in range(n_heads):
                lo = hh * ATT_HD
                y = _rms_rows(p[:, lo:lo + ATT_HD]) * gain
                r = y * cos + pltpu.roll(y, ATT_HD // 2, 1) * sin
                o_ref[rs, out_lo + lo:out_lo + lo + ATT_HD] = r.astype(BF16)

        norm_rope(_dot(h, w_ref[:, :qd]), ATT_HEADS, qg_ref[...] * ATT_EXP2_SCALE, 0)
        norm_rope(_dot(h, w_ref[:, qd:qd + kd]), ATT_KV_HEADS, kg_ref[...], qd)
        o_ref[rs, qd + kd:] = _dot(h, w_ref[:, qd + kd:]).astype(BF16)


def _proj_call(kernel, name, rows, x, mod, layer, gain, w, which, extras, extra_specs):
    d = x.shape[1]
    n_out = w.shape[-1]
    return pl.pallas_call(
        kernel,
        grid=(rows.n_all,),
        in_specs=[rows.row_spec(d), rows.mod_spec(layer, d), pl.BlockSpec((1, d), lambda i: (0, 0)),
                  _resident(w, (which,))] + extra_specs,
        out_specs=rows.row_spec(n_out),
        out_shape=jax.ShapeDtypeStruct((rows.n_all * rows.tile, n_out), BF16),
        compiler_params=_params(1),
        name=name,
    )(x, mod, gain.reshape(1, d), w, *extras)


def _decay_tables(n, lg_f, lg_b):
    i = lax.broadcasted_iota(jnp.int32, (n, n), 0)
    j = lax.broadcasted_iota(jnp.int32, (n, n), 1)
    diff = (i - j).astype(F32)
    mask = (jnp.where(diff >= 0, jnp.exp(jnp.maximum(diff, 0.0) * lg_f), 0.0)
            + jnp.where(diff <= 0, jnp.exp(jnp.maximum(-diff, 0.0) * lg_b), 0.0))
    pos = lax.broadcasted_iota(jnp.int32, (n, 1), 0).astype(F32)
    xi_f = jnp.exp((pos + 1.0) * lg_f)
    xi_b = jnp.exp((n - pos) * lg_b)
    zeta_f = jnp.exp((n - 1.0 - pos) * lg_f)
    zeta_b = jnp.exp(pos * lg_b)
    return mask, xi_f, xi_b, zeta_f, zeta_b


def _scaled_bf16(x, col):
    return (x.astype(F32) * col).astype(BF16)


def _head_norm_gate(o, g):
    return (_rms_rows(o) * _silu(g.astype(F32))).astype(BF16)


def _scaled_pair(x, col_a, col_b):
    xf = x.astype(F32)
    return jnp.concatenate([(xf * col_a).astype(BF16), (xf * col_b).astype(BF16)], axis=1)


def _ret_kernel(ql_ref, kl_ref, vl_ref, gl_ref, qc_ref, kc_ref, vc_ref, gc_ref, dec_ref,
                ol_ref, oc_ref, st_ref, ub_ref, runf_ref, runb_ref, mask_ref, cmask_ref,
                col_ref, ccol_ref, *, n_chunks, chunk, ctx_len):
    dk = RET_DK
    lg_f = -jnp.exp(dec_ref[0, 0:1, 0:1])
    lg_b = -jnp.exp(dec_ref[0, 1:2, 0:1])

    @pl.when(pl.program_id(1) == 0)
    def _():
        mask, xi_f, xi_b, zeta_f, zeta_b = _decay_tables(chunk, lg_f, lg_b)
        mask_ref[...] = mask
        for k, col in enumerate((xi_f, xi_b, zeta_f, zeta_b)):
            col_ref[k] = col
        cmask, _, _, czeta_f, czeta_b = _decay_tables(ctx_len, lg_f, lg_b)
        cmask_ref[...] = cmask
        ccol_ref[0] = czeta_f
        ccol_ref[1] = czeta_b

    decay_f = jnp.exp(chunk * lg_f)
    decay_b = jnp.exp(chunk * lg_b)

    q, k, v = qc_ref[...], kc_ref[...], vc_ref[...]
    p = (_dot_nt(q, k) * cmask_ref[...]).astype(BF16)
    oc_ref[...] = _head_norm_gate(_dot(p, v), gc_ref[...])
    s0 = _dot_tn(_scaled_pair(k, ccol_ref[0], ccol_ref[1]), v)
    runf_ref[...] = s0[:dk]
    runb_ref[...] = s0[dk:]
    st_ref[0, :dk, :] = s0[:dk].astype(BF16)
    st_ref[n_chunks - 1, dk:, :] = s0[dk:].astype(BF16)

    def rows(n):
        return slice(n * chunk, (n + 1) * chunk)

    for n in range(n_chunks):
        u = _dot_tn(_scaled_pair(kl_ref[rows(n), :], col_ref[2], col_ref[3]), vl_ref[rows(n), :])
        if n + 1 < n_chunks:
            s = runf_ref[...] * decay_f + u[:dk]
            runf_ref[...] = s
            st_ref[n + 1, :dk, :] = s.astype(BF16)
        if n > 0:
            ub_ref[n] = u[dk:]

    for n in range(n_chunks - 1, 0, -1):
        s = runb_ref[...] * decay_b + ub_ref[n]
        runb_ref[...] = s
        st_ref[n - 1, dk:, :] = s.astype(BF16)

    for n in range(n_chunks):
        r = rows(n)
        q, k, v = ql_ref[r, :], kl_ref[r, :], vl_ref[r, :]
        p = (_dot_nt(q, k) * mask_ref[...]).astype(BF16)
        o = _dot(p, v) + _dot(_scaled_pair(q, col_ref[0], col_ref[1]), st_ref[n])
        ol_ref[r, :] = _head_norm_gate(o, gl_ref[r, :])


def _ret_call(batch, seq, ctx_len, proj, dec_tab):
    chunk = min(RET_CHUNK, seq)
    assert seq % chunk == 0
    n_chunks = seq // chunk
    ctx_row0 = batch * seq // ctx_len
    kq = RET_HEADS
    kv = 2 * RET_HEADS * RET_DK // RET_DV
    kg = kv + RET_HEADS
    vd = RET_HEADS * RET_DV
    in_specs = [
        pl.BlockSpec((seq, RET_DK), lambda h, b: (b, h)),
        pl.BlockSpec((seq, RET_DK), lambda h, b: (b, kq + h)),
        pl.BlockSpec((seq, RET_DV), lambda h, b: (b, kv + h)),
        pl.BlockSpec((seq, RET_DV), lambda h, b: (b, kg + h)),
        pl.BlockSpec((ctx_len, RET_DK), lambda h, b: (ctx_row0 + b, h)),
        pl.BlockSpec((ctx_len, RET_DK), lambda h, b: (ctx_row0 + b, kq + h)),
        pl.BlockSpec((ctx_len, RET_DV), lambda h, b: (ctx_row0 + b, kv + h)),
        pl.BlockSpec((ctx_len, RET_DV), lambda h, b: (ctx_row0 + b, kg + h)),
        pl.BlockSpec((1, 8, 128), lambda h, b: (h, 0, 0)),
    ]
    return pl.pallas_call(
        functools.partial(_ret_kernel, n_chunks=n_chunks, chunk=chunk, ctx_len=ctx_len),
        grid=(RET_HEADS, batch),
        in_specs=in_specs,
        out_specs=[pl.BlockSpec((seq, RET_DV), lambda h, b: (b, h)),
                   pl.BlockSpec((ctx_len, RET_DV), lambda h, b: (b, h))],
        out_shape=[jax.ShapeDtypeStruct((batch * seq, vd), BF16),
                   jax.ShapeDtypeStruct((batch * ctx_len, vd), BF16)],
        scratch_shapes=[pltpu.VMEM((n_chunks, 2 * RET_DK, RET_DV), BF16),
                        pltpu.VMEM((n_chunks, RET_DK, RET_DV), F32),
                        pltpu.VMEM((RET_DK, RET_DV), F32),
                        pltpu.VMEM((RET_DK, RET_DV), F32),
                        pltpu.VMEM((chunk, chunk), F32),
                        pltpu.VMEM((ctx_len, ctx_len), F32),
                        pltpu.VMEM((4, chunk, 1), F32),
                        pltpu.VMEM((2, ctx_len, 1), F32)],
        compiler_params=_params(2),
        name="ret",
    )(proj, proj, proj, proj, proj, proj, proj, proj, dec_tab)


def _att_kernel(q_ref, kl_ref, vl_ref, kc_ref, vc_ref, o_ref, s_ref, p_ref, va_ref, *, n_q, seq):
    qi = pl.program_id(2)
    group = ATT_HEADS // ATT_KV_HEADS
    tq = q_ref.shape[0]
    n_keys = s_ref.shape[1]

    @pl.when(qi == 0)
    def _():
        va_ref[:seq, :ATT_HD] = vl_ref[...]
        va_ref[seq:, :ATT_HD] = vc_ref[...]
        va_ref[:, ATT_HD:] = jnp.ones((n_keys, ATT_HD), BF16)

    def attend(col_lo):
        slabs = range(col_lo, n_keys, LANES)
        for g in range(group):
            q = q_ref[:, g * ATT_HD:(g + 1) * ATT_HD]
            if col_lo < seq:
                s_ref[g * tq:(g + 1) * tq, :seq] = _dot_nt(q, kl_ref[...])
            s_ref[g * tq:(g + 1) * tq, seq:] = _dot_nt(q, kc_ref[...])
        for g in range(group):
            rs = slice(g * tq, (g + 1) * tq)
            m = None
            for c0 in slabs:
                t = s_ref[rs, c0:c0 + LANES]
                m = t if m is None else jnp.maximum(m, t)
            m = jnp.max(m, axis=-1, keepdims=True)
            o = None
            for k0 in range(col_lo, n_keys, ATT_KEY_CHUNK):
                k1 = min(k0 + ATT_KEY_CHUNK, seq if k0 < seq else n_keys)
                for c0 in range(k0, k1, LANES):
                    e = jnp.exp2(s_ref[rs, c0:c0 + LANES] - m)
                    p_ref[rs, c0:c0 + LANES] = e.astype(BF16)
                part = _dot(p_ref[rs, k0:k1], va_ref[k0:k1, :])
                o = part if o is None else o + part
            o_ref[:, g * ATT_HD:(g + 1) * ATT_HD] = (o[:, :ATT_HD] / o[:, ATT_HD:]).astype(BF16)

    @pl.when(qi < n_q)
    def _():
        attend(0)

    @pl.when(qi >= n_q)
    def _():
        attend(seq)


def _att_call(batch, seq, ctx_len, proj, ctx_out):
    tq = ATT_Q_TILE
    assert ctx_len == tq and seq % tq == 0
    n_q = seq // tq
    ctx_row0 = batch * seq // ctx_len
    group_w = (ATT_HEADS // ATT_KV_HEADS) * ATT_HD
    k_col = ATT_HEADS
    v_col = ATT_HEADS + ATT_KV_HEADS
    n_rows = batch * seq + (batch * ctx_len if ctx_out else 0)

    def q_map(b, kh, qi):
        return (jnp.where(qi < n_q, b * n_q + qi, ctx_row0 + b), kh)

    return pl.pallas_call(
        functools.partial(_att_kernel, n_q=n_q, seq=seq),
        grid=(batch, ATT_KV_HEADS, n_q + (1 if ctx_out else 0)),
        scratch_shapes=[pltpu.VMEM((group_w // ATT_HD * tq, seq + ctx_len), F32),
                        pltpu.VMEM((group_w // ATT_HD * tq, seq + ctx_len), BF16),
                        pltpu.VMEM((seq + ctx_len, 2 * ATT_HD), BF16)],
        in_specs=[
            pl.BlockSpec((tq, group_w), q_map),
            pl.BlockSpec((seq, ATT_HD), lambda b, kh, qi: (b, k_col + kh)),
            pl.BlockSpec((seq, ATT_HD), lambda b, kh, qi: (b, v_col + kh)),
            pl.BlockSpec((ctx_len, ATT_HD), lambda b, kh, qi: (ctx_row0 + b, k_col + kh)),
            pl.BlockSpec((ctx_len, ATT_HD), lambda b, kh, qi: (ctx_row0 + b, v_col + kh)),
        ],
        out_specs=pl.BlockSpec((tq, group_w), q_map),
        out_shape=jax.ShapeDtypeStruct((n_rows, ATT_HEADS * ATT_HD), BF16),
        compiler_params=_params(3),
        name="att",
    )(proj, proj, proj, proj, proj)


def _oproj_kernel(x_ref, mod_ref, w_ref, *rest, n_lat):
    gate = mod_ref[0, 0, 5:6, :]
    if len(rest) == 2:
        a_ref, o_ref = rest
        o_ref[...] = x_ref[...] + gate * _dot(a_ref[...], w_ref[...])
        return
    al_ref, ac_ref, o_ref = rest
    i = pl.program_id(0)

    @pl.when(i < n_lat)
    def _():
        o_ref[...] = x_ref[...] + gate * _dot(al_ref[...], w_ref[...])

    @pl.when(i >= n_lat)
    def _():
        o_ref[...] = x_ref[...] + gate * _dot(ac_ref[...], w_ref[...])


def _oproj_call(rows, latent_only, x, mod, layer, w, which, a_lat, a_ctx=None):
    d = x.shape[1]
    k = w.shape[-2]
    n_lat = rows.n_lat
    n_tiles = n_lat if latent_only else rows.n_all
    in_specs = [rows.row_spec(d), rows.mod_spec(layer, d), _resident(w, (which,))]
    args = [x, mod, w]
    if a_ctx is None:
        in_specs.append(rows.row_spec(k))
        args.append(a_lat)
    else:
        in_specs.append(pl.BlockSpec((rows.tile, k), lambda i: (jnp.minimum(i, n_lat - 1), 0)))
        in_specs.append(pl.BlockSpec((rows.tile, k), lambda i: (jnp.maximum(i - n_lat, 0), 0)))
        args += [a_lat, a_ctx]
    return pl.pallas_call(
        functools.partial(_oproj_kernel, n_lat=n_lat),
        grid=(n_tiles,),
        in_specs=in_specs,
        out_specs=rows.row_spec(d),
        out_shape=jax.ShapeDtypeStruct((n_tiles * rows.tile, d), F32),
        compiler_params=_params(1),
        name="oproj",
    )(*args)


def _with_identity_rows(cos, sin, pad_rows):
    pad = (pad_rows, cos.shape[1])
    return (jnp.concatenate([cos, jnp.ones(pad, F32)], axis=0),
            jnp.concatenate([sin, jnp.zeros(pad, F32)], axis=0))


def _seq_rope_tables(seq, pad_rows):
    half = RET_DK // 2
    freqs = ROPE_THETA ** (-jnp.arange(half, dtype=F32) / half)
    ang = jnp.arange(seq, dtype=F32)[:, None] * freqs
    return _with_identity_rows(jnp.cos(ang), jnp.sin(ang), pad_rows)


def _axial_rope_tables(seq, pad_rows):
    quarter = ATT_HD // 4
    tok = jnp.arange(seq)
    r = (tok // GRID_W).astype(F32)
    cl = (tok % GRID_W).astype(F32)
    freqs = ROPE_THETA ** (-jnp.arange(quarter, dtype=F32) / quarter)
    ang = jnp.concatenate([r[:, None] * freqs, cl[:, None] * freqs], axis=-1)
    cos, sin = jnp.cos(ang), jnp.sin(ang)
    return _with_identity_rows(jnp.concatenate([cos, cos], axis=-1),
                               jnp.concatenate([-sin, sin], axis=-1), pad_rows)


def kernel(x, c, ctx, c_ctx, ada_w, ada_b, norm_g, ffn_w1, ffn_w2, ret_w_in, ret_w_out,
           ret_decay_f, ret_decay_b, att_w_qkv, att_w_o, att_q_gain, att_k_gain, final_g):
    batch, seq, d = x.shape
    ctx_len = ctx.shape[1]
    depth = ada_w.shape[0]
    rows = _Rows(batch, seq, ctx_len)
    rows_ffn = _Rows.widest(batch, seq, ctx_len, FFN_SUBTILES)

    cc = jnp.concatenate(
        [c, c_ctx[None], jnp.zeros((ADA_ROWS - batch - 1, d), F32)], axis=0)
    mod = _ada_call(cc, ada_w, ada_b).reshape(depth, ADA_ROWS, N_MOD, d)

    w1 = ffn_w1.astype(BF16)
    w2 = ffn_w2.astype(BF16)
    w_in = ret_w_in.astype(BF16)
    w_out = ret_w_out.astype(BF16)
    w_qkv = att_w_qkv.astype(BF16)
    w_o = att_w_o.astype(BF16)

    rows_att = _Rows.widest(batch, seq, ctx_len, PROJ_ATT_SUBTILES)
    ret_cos, ret_sin = _seq_rope_tables(seq, rows.tile)
    att_cos, att_sin = _axial_rope_tables(seq, rows_att.tile)
    ret_rope_specs = [rows.rope_spec(RET_DK // 2)] * 2
    att_rope_specs = [rows_att.rope_spec(ATT_HD)] * 2
    gain_spec = pl.BlockSpec((1, ATT_HD), lambda i: (0, 0))

    xs = x.reshape(batch * seq, d)
    xs_ctx = ctx.reshape(batch * ctx_len, d)

    for i in range(depth):
        last = i == depth - 1
        j = i // 2
        xs = _ffn_call(rows_ffn, False, xs, mod, i, 0, norm_g[i, 0], w1, w2, 0,
                       x_ctx=xs_ctx if i == 0 else None)
        if i % 2 == 0:
            proj = _proj_call(_proj_ret_kernel, "proj_ret", rows, xs, mod, i, norm_g[i, 1],
                              w_in, j, [ret_cos, ret_sin], ret_rope_specs)
            dec_tab = jnp.broadcast_to(
                jnp.stack([ret_decay_f[j], ret_decay_b[j]], axis=1)[:, :, None],
                (RET_HEADS, 2, 128))
            dec_tab = jnp.concatenate([dec_tab, jnp.zeros((RET_HEADS, 6, 128), F32)], axis=1)
            a_lat, a_ctx = _ret_call(batch, seq, ctx_len, proj, dec_tab)
            xs = _oproj_call(rows_ffn, last, xs, mod, i, w_out, j, a_lat,
                             None if last else a_ctx)
            mixer = None
        else:
            proj = _proj_call(_proj_att_kernel, "proj_att", rows_att, xs, mod, i, norm_g[i, 1],
                              w_qkv, j,
                              [att_q_gain[j].reshape(1, ATT_HD), att_k_gain[j].reshape(1, ATT_HD),
                               att_cos, att_sin],
                              [gain_spec, gain_spec] + att_rope_specs)
            mixer = (_att_call(batch, seq, ctx_len, proj, not last), w_o, j)
        xs = _ffn_call(rows_ffn, last, xs, mod, i, 6, norm_g[i, 2], w1, w2, 1,
                       final_g if last else None, mixer=mixer)
    return xs.reshape(batch, seq, d)
```

```python
import functools

import jax
import jax.numpy as jnp
from jax import lax
from jax.experimental import pallas as pl
from jax.experimental.pallas import tpu as pltpu

F32 = jnp.float32
BF16 = jnp.bfloat16

N_MOD = 9
RET_HEADS = 4
RET_DK = 256
RET_DV = 512
ATT_HEADS = 8
ATT_KV_HEADS = 2
ATT_HD = 128
GRID_W = 64
ROPE_THETA = 10000.0
EPS = 1e-6

ROW_TILE = 512
RET_CHUNK = 256
ATT_Q_TILE = 256
ATT_KEY_CHUNK = 512
ATT_EXP2_SCALE = ATT_HD ** -0.5 * 1.4426950408889634
LANES = 128
MXU_DEPTH = 256
FFN_SPLIT = 3
FFN_SUBTILES = 2
FFN_MATMUL_ROWS = 1024
PROJ_ATT_SUBTILES = 2
ADA_ROWS = 16
VMEM_LIMIT = 58 * 1024 * 1024


def _params(n_axes):
    return pltpu.CompilerParams(
        dimension_semantics=("arbitrary",) * n_axes, vmem_limit_bytes=VMEM_LIMIT)


def _resident(stacked, lead):
    tail = stacked.shape[len(lead):]
    index = tuple(lead) + (0,) * len(tail)
    return pl.BlockSpec((None,) * len(lead) + tail, lambda *_: index,
                        pipeline_mode=pl.Buffered(1))


def _silu(x):
    return x * jax.nn.sigmoid(x)


def _rms_rows(x):
    return x * lax.rsqrt(jnp.mean(x * x, axis=-1, keepdims=True) + EPS)


def _modulated(x, gain, shift, scale):
    return (_rms_rows(x) * gain) * (1.0 + scale) + shift


def _dot(a, b):
    return jnp.dot(a, b, preferred_element_type=F32)


def _dot_nt(a, b):
    return lax.dot_general(a, b, (((1,), (1,)), ((), ())), preferred_element_type=F32)


def _dot_tn(a, b):
    return lax.dot_general(a, b, (((0,), (0,)), ((), ())), preferred_element_type=F32)


def _ada_kernel(cc_ref, w_ref, b_ref, o_ref):
    s = _silu(cc_ref[...])
    o_ref[0] = _dot(s.astype(BF16), w_ref[0].astype(BF16)) + b_ref[0]


def _ada_call(cc, ada_w, ada_b):
    depth, d, n = ada_w.shape
    tn = 1024
    return pl.pallas_call(
        _ada_kernel,
        grid=(depth, n // tn),
        in_specs=[
            pl.BlockSpec((ADA_ROWS, d), lambda l, j: (0, 0)),
            pl.BlockSpec((1, d, tn), lambda l, j: (l, 0, j)),
            pl.BlockSpec((1, 1, tn), lambda l, j: (l, 0, j)),
        ],
        out_specs=pl.BlockSpec((1, ADA_ROWS, tn), lambda l, j: (l, 0, j)),
        out_shape=jax.ShapeDtypeStruct((depth, ADA_ROWS, n), F32),
        compiler_params=_params(2),
        name="ada",
    )(cc, ada_w, ada_b.reshape(depth, 1, n))


class _Rows:
    def __init__(self, batch, seq, ctx_len, tile=ROW_TILE):
        self.batch, self.seq, self.ctx_len, self.tile = batch, seq, ctx_len, tile
        assert seq % tile == 0 and (batch * ctx_len) % tile == 0
        self.tiles_per_batch = seq // tile
        self.n_lat = batch * seq // tile
        self.n_ctx = batch * ctx_len // tile
        self.n_all = self.n_lat + self.n_ctx

    @classmethod
    def widest(cls, batch, seq, ctx_len, factor):
        tile = ROW_TILE * factor
        if seq % tile or (batch * ctx_len) % tile:
            tile = ROW_TILE
        return cls(batch, seq, ctx_len, tile)

    def mod_spec(self, layer, d):
        n_lat, tpb, batch = self.n_lat, self.tiles_per_batch, self.batch
        return pl.BlockSpec(
            (1, 1, N_MOD, d),
            lambda i: (layer, jnp.where(i < n_lat, i // tpb, batch), 0, 0))

    def rope_spec(self, width):
        n_lat, tpb = self.n_lat, self.tiles_per_batch
        return pl.BlockSpec((self.tile, width), lambda i: (jnp.where(i < n_lat, i % tpb, tpb), 0))

    def row_spec(self, width):
        return pl.BlockSpec((self.tile, width), lambda i: (i, 0))


def _mod_rows(mod_ref, base):
    return [mod_ref[0, 0, base + k:base + k + 1, :] for k in range(3)]


def _ffn_kernel(*refs, base, ffn_dim, chunks, final, n_lat_split, mixer):
    refs = list(refs)
    o_ref = refs.pop()
    fg_ref = refs.pop() if final else None
    x_ref = refs.pop(0)
    xc_ref = refs.pop(0) if n_lat_split is not None else None
    a_ref, wm_ref = (refs.pop(0), refs.pop(0)) if mixer else (None, None)
    mod_ref, g_ref, w1_ref, w2_ref = refs
    shift, scale, gate = _mod_rows(mod_ref, base)
    sub = min(FFN_MATMUL_ROWS, o_ref.shape[0])
    for r0 in range(0, o_ref.shape[0], sub):
        rs = slice(r0, r0 + sub)
        x = x_ref[rs, :]
        if xc_ref is not None:
            x = jnp.where(pl.program_id(0) < n_lat_split, x, xc_ref[rs, :])
        if mixer:
            x = x + mod_ref[0, 0, 5:6, :] * _dot(a_ref[rs, :], wm_ref[...])
        h = _modulated(x, g_ref[...], shift, scale).astype(BF16)
        acc = None
        lo = 0
        for width in chunks:
            gt = _dot(h, w1_ref[:, lo:lo + width])
            up = _dot(h, w1_ref[:, ffn_dim + lo:ffn_dim + lo + width])
            a = (_silu(gt) * up).astype(BF16)
            part = _dot(a, w2_ref[lo:lo + width, :])
            acc = part if acc is None else acc + part
            lo += width
        y = x + (0.5 * gate) * acc
        if final:
            y = _rms_rows(y) * fg_ref[...]
        o_ref[rs, :] = y


def _ffn_chunks(ffn_dim):
    assert ffn_dim % MXU_DEPTH == 0
    n_tiles = ffn_dim // MXU_DEPTH
    first = (n_tiles // FFN_SPLIT) * MXU_DEPTH
    sizes = [first] * (FFN_SPLIT - 1) + [ffn_dim - first * (FFN_SPLIT - 1)]
    return tuple(s for s in sizes if s)


def _ffn_call(rows, latent_only, x, mod, layer, base, gain, w1, w2, which, final_g=None,
              x_ctx=None, mixer=None):
    d = x.shape[1]
    ffn_dim = w2.shape[-2]
    chunks = _ffn_chunks(ffn_dim)
    final = final_g is not None
    n_lat = rows.n_lat
    n_tiles = n_lat if latent_only else rows.n_all
    if x_ctx is None:
        in_specs, args = [rows.row_spec(d)], [x]
    else:
        in_specs = [pl.BlockSpec((rows.tile, d), lambda i: (jnp.minimum(i, n_lat - 1), 0)),
                    pl.BlockSpec((rows.tile, d), lambda i: (jnp.maximum(i - n_lat, 0), 0))]
        args = [x, x_ctx]
    if mixer is not None:
        a, w_mix, mix_idx = mixer
        in_specs += [rows.row_spec(a.shape[1]), _resident(w_mix, (mix_idx,))]
        args += [a, w_mix]
    in_specs += [rows.mod_spec(layer, d), pl.BlockSpec((1, d), lambda i: (0, 0)),
                 _resident(w1, (layer, which)), _resident(w2, (layer, which))]
    args += [mod, gain.reshape(1, d), w1, w2]
    if final:
        in_specs.append(pl.BlockSpec((1, d), lambda i: (0, 0)))
        args.append(final_g.reshape(1, d))
    return pl.pallas_call(
        functools.partial(_ffn_kernel, base=base, ffn_dim=ffn_dim, chunks=chunks, final=final,
                          n_lat_split=None if x_ctx is None else n_lat,
                          mixer=mixer is not None),
        grid=(n_tiles,),
        in_specs=in_specs,
        out_specs=rows.row_spec(d),
        out_shape=jax.ShapeDtypeStruct((n_tiles * rows.tile, d), F32),
        compiler_params=_params(1),
        name="ffn",
    )(*args)


def _rope_halves(x1, x2, cos, sin):
    return x1 * cos - x2 * sin, x2 * cos + x1 * sin


def _proj_ret_kernel(x_ref, mod_ref, g_ref, w_ref, cos_ref, sin_ref, o_ref):
    shift, scale, _ = _mod_rows(mod_ref, 3)
    h = _modulated(x_ref[...], g_ref[...], shift, scale).astype(BF16)
    cos, sin = cos_ref[...], sin_ref[...]
    qd = RET_HEADS * RET_DK
    vd = RET_HEADS * RET_DV
    half = RET_DK // 2
    for part, mult in ((0, None), (1, RET_DK ** -0.5)):
        p = _dot(h, w_ref[:, part * qd:(part + 1) * qd])
        if mult is not None:
            p = p * mult
        for hh in range(RET_HEADS):
            lo = hh * RET_DK
            r1, r2 = _rope_halves(p[:, lo:lo + half], p[:, lo + half:lo + RET_DK], cos, sin)
            o_ref[:, part * qd + lo:part * qd + lo + half] = r1.astype(BF16)
            o_ref[:, part * qd + lo + half:part * qd + lo + RET_DK] = r2.astype(BF16)
    for part in range(2):
        lo = 2 * qd + part * vd
        o_ref[:, lo:lo + vd] = _dot(h, w_ref[:, lo:lo + vd]).astype(BF16)


def _proj_att_kernel(x_ref, mod_ref, g_ref, w_ref, qg_ref, kg_ref, cos_ref, sin_ref, o_ref):
    shift, scale, _ = _mod_rows(mod_ref, 3)
    qd = ATT_HEADS * ATT_HD
    kd = ATT_KV_HEADS * ATT_HD
    for r0 in range(0, o_ref.shape[0], ROW_TILE):
        rs = slice(r0, r0 + ROW_TILE)
        h = _modulated(x_ref[rs, :], g_ref[...], shift, scale).astype(BF16)
        cos, sin = cos_ref[rs, :], sin_ref[rs, :]

        def norm_rope(p, n_heads, gain, out_lo):
            for hh in range(n_heads):
                lo = hh * ATT_HD
                y = _rms_rows(p[:, lo:lo + ATT_HD]) * gain
                r = y * cos + pltpu.roll(y, ATT_HD // 2, 1) * sin
                o_ref[rs, out_lo + lo:out_lo + lo + ATT_HD] = r.astype(BF16)

        norm_rope(_dot(h, w_ref[:, :qd]), ATT_HEADS, qg_ref[...] * ATT_EXP2_SCALE, 0)
        norm_rope(_dot(h, w_ref[:, qd:qd + kd]), ATT_KV_HEADS, kg_ref[...], qd)
        o_ref[rs, qd + kd:] = _dot(h, w_ref[:, qd + kd:]).astype(BF16)


def _proj_call(kernel, name, rows, x, mod, layer, gain, w, which, extras, extra_specs):
    d = x.shape[1]
    n_out = w.shape[-1]
    return pl.pallas_call(
        kernel,
        grid=(rows.n_all,),
        in_specs=[rows.row_spec(d), rows.mod_spec(layer, d), pl.BlockSpec((1, d), lambda i: (0, 0)),
                  _resident(w, (which,))] + extra_specs,
        out_specs=rows.row_spec(n_out),
        out_shape=jax.ShapeDtypeStruct((rows.n_all * rows.tile, n_out), BF16),
        compiler_params=_params(1),
        name=name,
    )(x, mod, gain.reshape(1, d), w, *extras)


def _decay_tables(n, lg_f, lg_b):
    i = lax.broadcasted_iota(jnp.int32, (n, n), 0)
    j = lax.broadcasted_iota(jnp.int32, (n, n), 1)
    diff = (i - j).astype(F32)
    mask = (jnp.where(diff >= 0, jnp.exp(jnp.maximum(diff, 0.0) * lg_f), 0.0)
            + jnp.where(diff <= 0, jnp.exp(jnp.maximum(-diff, 0.0) * lg_b), 0.0))
    pos = lax.broadcasted_iota(jnp.int32, (n, 1), 0).astype(F32)
    xi_f = jnp.exp((pos + 1.0) * lg_f)
    xi_b = jnp.exp((n - pos) * lg_b)
    zeta_f = jnp.exp((n - 1.0 - pos) * lg_f)
    zeta_b = jnp.exp(pos * lg_b)
    return mask, xi_f, xi_b, zeta_f, zeta_b


def _scaled_bf16(x, col):
    return (x.astype(F32) * col).astype(BF16)


def _head_norm_gate(o, g):
    return (_rms_rows(o) * _silu(g.astype(F32))).astype(BF16)


def _scaled_pair(x, col_a, col_b):
    xf = x.astype(F32)
    return jnp.concatenate([(xf * col_a).astype(BF16), (xf * col_b).astype(BF16)], axis=1)


def _ret_kernel(ql_ref, kl_ref, vl_ref, gl_ref, qc_ref, kc_ref, vc_ref, gc_ref, dec_ref,
                ol_ref, oc_ref, st_ref, ub_ref, runf_ref, runb_ref, mask_ref, cmask_ref,
                col_ref, ccol_ref, *, n_chunks, chunk, ctx_len):
    dk = RET_DK
    lg_f = -jnp.exp(dec_ref[0, 0:1, 0:1])
    lg_b = -jnp.exp(dec_ref[0, 1:2, 0:1])

    @pl.when(pl.program_id(1) == 0)
    def _():
        mask, xi_f, xi_b, zeta_f, zeta_b = _decay_tables(chunk, lg_f, lg_b)
        mask_ref[...] = mask
        for k, col in enumerate((xi_f, xi_b, zeta_f, zeta_b)):
            col_ref[k] = col
        cmask, _, _, czeta_f, czeta_b = _decay_tables(ctx_len, lg_f, lg_b)
        cmask_ref[...] = cmask
        ccol_ref[0] = czeta_f
        ccol_ref[1] = czeta_b

    decay_f = jnp.exp(chunk * lg_f)
    decay_b = jnp.exp(chunk * lg_b)

    q, k, v = qc_ref[...], kc_ref[...], vc_ref[...]
    p = (_dot_nt(q, k) * cmask_ref[...]).astype(BF16)
    oc_ref[...] = _head_norm_gate(_dot(p, v), gc_ref[...])
    s0 = _dot_tn(_scaled_pair(k, ccol_ref[0], ccol_ref[1]), v)
    runf_ref[...] = s0[:dk]
    runb_ref[...] = s0[dk:]
    st_ref[0, :dk, :] = s0[:dk].astype(BF16)
    st_ref[n_chunks - 1, dk:, :] = s0[dk:].astype(BF16)

    def rows(n):
        return slice(n * chunk, (n + 1) * chunk)

    for n in range(n_chunks):
        u = _dot_tn(_scaled_pair(kl_ref[rows(n), :], col_ref[2], col_ref[3]), vl_ref[rows(n), :])
        if n + 1 < n_chunks:
            s = runf_ref[...] * decay_f + u[:dk]
            runf_ref[...] = s
            st_ref[n + 1, :dk, :] = s.astype(BF16)
        if n > 0:
            ub_ref[n] = u[dk:]

    for n in range(n_chunks - 1, 0, -1):
        s = runb_ref[...] * decay_b + ub_ref[n]
        runb_ref[...] = s
        st_ref[n - 1, dk:, :] = s.astype(BF16)

    for n in range(n_chunks):
        r = rows(n)
        q, k, v = ql_ref[r, :], kl_ref[r, :], vl_ref[r, :]
        p = (_dot_nt(q, k) * mask_ref[...]).astype(BF16)
        o = _dot(p, v) + _dot(_scaled_pair(q, col_ref[0], col_ref[1]), st_ref[n])
        ol_ref[r, :] = _head_norm_gate(o, gl_ref[r, :])


def _ret_call(batch, seq, ctx_len, proj, dec_tab):
    chunk = min(RET_CHUNK, seq)
    assert seq % chunk == 0
    n_chunks = seq // chunk
    ctx_row0 = batch * seq // ctx_len
    kq = RET_HEADS
    kv = 2 * RET_HEADS * RET_DK // RET_DV
    kg = kv + RET_HEADS
    vd = RET_HEADS * RET_DV
    in_specs = [
        pl.BlockSpec((seq, RET_DK), lambda h, b: (b, h)),
        pl.BlockSpec((seq, RET_DK), lambda h, b: (b, kq + h)),
        pl.BlockSpec((seq, RET_DV), lambda h, b: (b, kv + h)),
        pl.BlockSpec((seq, RET_DV), lambda h, b: (b, kg + h)),
        pl.BlockSpec((ctx_len, RET_DK), lambda h, b: (ctx_row0 + b, h)),
        pl.BlockSpec((ctx_len, RET_DK), lambda h, b: (ctx_row0 + b, kq + h)),
        pl.BlockSpec((ctx_len, RET_DV), lambda h, b: (ctx_row0 + b, kv + h)),
        pl.BlockSpec((ctx_len, RET_DV), lambda h, b: (ctx_row0 + b, kg + h)),
        pl.BlockSpec((1, 8, 128), lambda h, b: (h, 0, 0)),
    ]
    return pl.pallas_call(
        functools.partial(_ret_kernel, n_chunks=n_chunks, chunk=chunk, ctx_len=ctx_len),
        grid=(RET_HEADS, batch),
        in_specs=in_specs,
        out_specs=[pl.BlockSpec((seq, RET_DV), lambda h, b: (b, h)),
                   pl.BlockSpec((ctx_len, RET_DV), lambda h, b: (b, h))],
        out_shape=[jax.ShapeDtypeStruct((batch * seq, vd), BF16),
                   jax.ShapeDtypeStruct((batch * ctx_len, vd), BF16)],
        scratch_shapes=[pltpu.VMEM((n_chunks, 2 * RET_DK, RET_DV), BF16),
                        pltpu.VMEM((n_chunks, RET_DK, RET_DV), F32),
                        pltpu.VMEM((RET_DK, RET_DV), F32),
                        pltpu.VMEM((RET_DK, RET_DV), F32),
                        pltpu.VMEM((chunk, chunk), F32),
                        pltpu.VMEM((ctx_len, ctx_len), F32),
                        pltpu.VMEM((4, chunk, 1), F32),
                        pltpu.VMEM((2, ctx_len, 1), F32)],
        compiler_params=_params(2),
        name="ret",
    )(proj, proj, proj, proj, proj, proj, proj, proj, dec_tab)


def _att_kernel(q_ref, kl_ref, vl_ref, kc_ref, vc_ref, o_ref, s_ref, p_ref, va_ref, *, n_q, seq):
    qi = pl.program_id(2)
    group = ATT_HEADS // ATT_KV_HEADS
    tq = q_ref.shape[0]
    n_keys = s_ref.shape[1]

    @pl.when(qi == 0)
    def _():
        va_ref[:seq, :ATT_HD] = vl_ref[...]
        va_ref[seq:, :ATT_HD] = vc_ref[...]
        va_ref[:, ATT_HD:] = jnp.ones((n_keys, ATT_HD), BF16)

    def attend(col_lo):
        slabs = range(col_lo, n_keys, LANES)
        for g in range(group):
            q = q_ref[:, g * ATT_HD:(g + 1) * ATT_HD]
            if col_lo < seq:
                s_ref[g * tq:(g + 1) * tq, :seq] = _dot_nt(q, kl_ref[...])
            s_ref[g * tq:(g + 1) * tq, seq:] = _dot_nt(q, kc_ref[...])
        for g in range(group):
            rs = slice(g * tq, (g + 1) * tq)
            m = None
            for c0 in slabs:
                t = s_ref[rs, c0:c0 + LANES]
                m = t if m is None else jnp.maximum(m, t)
            m = jnp.max(m, axis=-1, keepdims=True)
            o = None
            for k0 in range(col_lo, n_keys, ATT_KEY_CHUNK):
                k1 = min(k0 + ATT_KEY_CHUNK, seq if k0 < seq else n_keys)
                for c0 in range(k0, k1, LANES):
                    e = jnp.exp2(s_ref[rs, c0:c0 + LANES] - m)
                    p_ref[rs, c0:c0 + LANES] = e.astype(BF16)
                part = _dot(p_ref[rs, k0:k1], va_ref[k0:k1, :])
                o = part if o is None else o + part
            o_ref[:, g * ATT_HD:(g + 1) * ATT_HD] = (o[:, :ATT_HD] / o[:, ATT_HD:]).astype(BF16)

    @pl.when(qi < n_q)
    def _():
        attend(0)

    @pl.when(qi >= n_q)
    def _():
        attend(seq)


def _att_call(batch, seq, ctx_len, proj, ctx_out):
    tq = ATT_Q_TILE
    assert ctx_len == tq and seq % tq == 0
    n_q = seq // tq
    ctx_row0 = batch * seq // ctx_len
    group_w = (ATT_HEADS // ATT_KV_HEADS) * ATT_HD
    k_col = ATT_HEADS
    v_col = ATT_HEADS + ATT_KV_HEADS
    n_rows = batch * seq + (batch * ctx_len if ctx_out else 0)

    def q_map(b, kh, qi):
        return (jnp.where(qi < n_q, b * n_q + qi, ctx_row0 + b), kh)

    return pl.pallas_call(
        functools.partial(_att_kernel, n_q=n_q, seq=seq),
        grid=(batch, ATT_KV_HEADS, n_q + (1 if ctx_out else 0)),
        scratch_shapes=[pltpu.VMEM((group_w // ATT_HD * tq, seq + ctx_len), F32),
                        pltpu.VMEM((group_w // ATT_HD * tq, seq + ctx_len), BF16),
                        pltpu.VMEM((seq + ctx_len, 2 * ATT_HD), BF16)],
        in_specs=[
            pl.BlockSpec((tq, group_w), q_map),
            pl.BlockSpec((seq, ATT_HD), lambda b, kh, qi: (b, k_col + kh)),
            pl.BlockSpec((seq, ATT_HD), lambda b, kh, qi: (b, v_col + kh)),
            pl.BlockSpec((ctx_len, ATT_HD), lambda b, kh, qi: (ctx_row0 + b, k_col + kh)),
            pl.BlockSpec((ctx_len, ATT_HD), lambda b, kh, qi: (ctx_row0 + b, v_col + kh)),
        ],
        out_specs=pl.BlockSpec((tq, group_w), q_map),
        out_shape=jax.ShapeDtypeStruct((n_rows, ATT_HEADS * ATT_HD), BF16),
        compiler_params=_params(3),
        name="att",
    )(proj, proj, proj, proj, proj)


def _oproj_kernel(x_ref, mod_ref, w_ref, *rest, n_lat):
    gate = mod_ref[0, 0, 5:6, :]
    if len(rest) == 2:
        a_ref, o_ref = rest
        o_ref[...] = x_ref[...] + gate * _dot(a_ref[...], w_ref[...])
        return
    al_ref, ac_ref, o_ref = rest
    i = pl.program_id(0)

    @pl.when(i < n_lat)
    def _():
        o_ref[...] = x_ref[...] + gate * _dot(al_ref[...], w_ref[...])

    @pl.when(i >= n_lat)
    def _():
        o_ref[...] = x_ref[...] + gate * _dot(ac_ref[...], w_ref[...])


def _oproj_call(rows, latent_only, x, mod, layer, w, which, a_lat, a_ctx=None):
    d = x.shape[1]
    k = w.shape[-2]
    n_lat = rows.n_lat
    n_tiles = n_lat if latent_only else rows.n_all
    in_specs = [rows.row_spec(d), rows.mod_spec(layer, d), _resident(w, (which,))]
    args = [x, mod, w]
    if a_ctx is None:
        in_specs.append(rows.row_spec(k))
        args.append(a_lat)
    else:
        in_specs.append(pl.BlockSpec((rows.tile, k), lambda i: (jnp.minimum(i, n_lat - 1), 0)))
        in_specs.append(pl.BlockSpec((rows.tile, k), lambda i: (jnp.maximum(i - n_lat, 0), 0)))
        args += [a_lat, a_ctx]
    return pl.pallas_call(
        functools.partial(_oproj_kernel, n_lat=n_lat),
        grid=(n_tiles,),
        in_specs=in_specs,
        out_specs=rows.row_spec(d),
        out_shape=jax.ShapeDtypeStruct((n_tiles * rows.tile, d), F32),
        compiler_params=_params(1),
        name="oproj",
    )(*args)


def _with_identity_rows(cos, sin, pad_rows):
    pad = (pad_rows, cos.shape[1])
    return (jnp.concatenate([cos, jnp.ones(pad, F32)], axis=0),
            jnp.concatenate([sin, jnp.zeros(pad, F32)], axis=0))


def _seq_rope_tables(seq, pad_rows):
    half = RET_DK // 2
    freqs = ROPE_THETA ** (-jnp.arange(half, dtype=F32) / half)
    ang = jnp.arange(seq, dtype=F32)[:, None] * freqs
    return _with_identity_rows(jnp.cos(ang), jnp.sin(ang), pad_rows)


def _axial_rope_tables(seq, pad_rows):
    quarter = ATT_HD // 4
    tok = jnp.arange(seq)
    r = (tok // GRID_W).astype(F32)
    cl = (tok % GRID_W).astype(F32)
    freqs = ROPE_THETA ** (-jnp.arange(quarter, dtype=F32) / quarter)
    ang = jnp.concatenate([r[:, None] * freqs, cl[:, None] * freqs], axis=-1)
    cos, sin = jnp.cos(ang), jnp.sin(ang)
    return _with_identity_rows(jnp.concatenate([cos, cos], axis=-1),
                               jnp.concatenate([-sin, sin], axis=-1), pad_rows)


def kernel(x, c, ctx, c_ctx, ada_w, ada_b, norm_g, ffn_w1, ffn_w2, ret_w_in, ret_w_out,
           ret_decay_f, ret_decay_b, att_w_qkv, att_w_o, att_q_gain, att_k_gain, final_g):
    batch, seq, d = x.shape
    ctx_len = ctx.shape[1]
    depth = ada_w.shape[0]
    rows = _Rows(batch, seq, ctx_len)
    rows_ffn = _Rows.widest(batch, seq, ctx_len, FFN_SUBTILES)

    cc = jnp.concatenate(
        [c, c_ctx[None], jnp.zeros((ADA_ROWS - batch - 1, d), F32)], axis=0)
    mod = _ada_call(cc, ada_w, ada_b).reshape(depth, ADA_ROWS, N_MOD, d)

    w1 = ffn_w1.astype(BF16)
    w2 = ffn_w2.astype(BF16)
    w_in = ret_w_in.astype(BF16)
    w_out = ret_w_out.astype(BF16)
    w_qkv = att_w_qkv.astype(BF16)
    w_o = att_w_o.astype(BF16)

    rows_att = _Rows.widest(batch, seq, ctx_len, PROJ_ATT_SUBTILES)
    ret_cos, ret_sin = _seq_rope_tables(seq, rows.tile)
    att_cos, att_sin = _axial_rope_tables(seq, rows_att.tile)
    ret_rope_specs = [rows.rope_spec(RET_DK // 2)] * 2
    att_rope_specs = [rows_att.rope_spec(ATT_HD)] * 2
    gain_spec = pl.BlockSpec((1, ATT_HD), lambda i: (0, 0))

    xs = x.reshape(batch * seq, d)
    xs_ctx = ctx.reshape(batch * ctx_len, d)

    for i in range(depth):
        last = i == depth - 1
        j = i // 2
        xs = _ffn_call(rows_ffn, False, xs, mod, i, 0, norm_g[i, 0], w1, w2, 0,
                       x_ctx=xs_ctx if i == 0 else None)
        if i % 2 == 0:
            proj = _proj_call(_proj_ret_kernel, "proj_ret", rows, xs, mod, i, norm_g[i, 1],
                              w_in, j, [ret_cos, ret_sin], ret_rope_specs)
            dec_tab = jnp.broadcast_to(
                jnp.stack([ret_decay_f[j], ret_decay_b[j]], axis=1)[:, :, None],
                (RET_HEADS, 2, 128))
            dec_tab = jnp.concatenate([dec_tab, jnp.zeros((RET_HEADS, 6, 128), F32)], axis=1)
            a_lat, a_ctx = _ret_call(batch, seq, ctx_len, proj, dec_tab)
            xs = _oproj_call(rows_ffn, last, xs, mod, i, w_out, j, a_lat,
                             None if last else a_ctx)
            mixer = None
        else:
            proj = _proj_call(_proj_att_kernel, "proj_att", rows_att, xs, mod, i, norm_g[i, 1],
                              w_qkv, j,
                              [att_q_gain[j].reshape(1, ATT_HD), att_k_gain[j].reshape(1, ATT_HD),
                               att_cos, att_sin],
                              [gain_spec, gain_spec] + att_rope_specs)
            mixer = (_att_call(batch, seq, ctx_len, proj, not last), w_o, j)
        xs = _ffn_call(rows_ffn, last, xs, mod, i, 6, norm_g[i, 2], w1, w2, 1,
                       final_g if last else None, mixer=mixer)
    return xs.reshape(batch, seq, d)
```

```python
import functools

import jax
import jax.numpy as jnp
from jax import lax
from jax.experimental import pallas as pl
from jax.experimental.pallas import tpu as pltpu

F32 = jnp.float32
BF16 = jnp.bfloat16

N_MOD = 9
RET_HEADS = 4
RET_DK = 256
RET_DV = 512
ATT_HEADS = 8
ATT_KV_HEADS = 2
ATT_HD = 128
GRID_W = 64
ROPE_THETA = 10000.0
EPS = 1e-6

ROW_TILE = 512
RET_CHUNK = 256
ATT_Q_TILE = 256
ATT_KEY_CHUNK = 512
ATT_EXP2_SCALE = ATT_HD ** -0.5 * 1.4426950408889634
LANES = 128
MXU_DEPTH = 256
FFN_SPLIT = 3
WIDE_TILE_FACTOR = 2
PROJ_ATT_SUBTILES = 2
ADA_ROWS = 16
VMEM_LIMIT = 58 * 1024 * 1024


def _params(n_axes):
    return pltpu.CompilerParams(
        dimension_semantics=("arbitrary",) * n_axes, vmem_limit_bytes=VMEM_LIMIT)


def _resident(stacked, lead):
    tail = stacked.shape[len(lead):]
    index = tuple(lead) + (0,) * len(tail)
    return pl.BlockSpec((None,) * len(lead) + tail, lambda *_: index,
                        pipeline_mode=pl.Buffered(1))


def _silu(x):
    return x * jax.nn.sigmoid(x)


def _rms_rows(x):
    return x * lax.rsqrt(jnp.mean(x * x, axis=-1, keepdims=True) + EPS)


def _modulated(x, gain, shift, scale):
    return (_rms_rows(x) * gain) * (1.0 + scale) + shift


def _dot(a, b):
    return jnp.dot(a, b, preferred_element_type=F32)


def _dot_nt(a, b):
    return lax.dot_general(a, b, (((1,), (1,)), ((), ())), preferred_element_type=F32)


def _dot_tn(a, b):
    return lax.dot_general(a, b, (((0,), (0,)), ((), ())), preferred_element_type=F32)


def _ada_kernel(cc_ref, w_ref, b_ref, o_ref):
    s = _silu(cc_ref[...])
    o_ref[0] = _dot(s.astype(BF16), w_ref[0].astype(BF16)) + b_ref[0]


def _ada_call(cc, ada_w, ada_b):
    depth, d, n = ada_w.shape
    tn = 1024
    return pl.pallas_call(
        _ada_kernel,
        grid=(depth, n // tn),
        in_specs=[
            pl.BlockSpec((ADA_ROWS, d), lambda l, j: (0, 0)),
            pl.BlockSpec((1, d, tn), lambda l, j: (l, 0, j)),
            pl.BlockSpec((1, 1, tn), lambda l, j: (l, 0, j)),
        ],
        out_specs=pl.BlockSpec((1, ADA_ROWS, tn), lambda l, j: (l, 0, j)),
        out_shape=jax.ShapeDtypeStruct((depth, ADA_ROWS, n), F32),
        compiler_params=_params(2),
        name="ada",
    )(cc, ada_w, ada_b.reshape(depth, 1, n))


class _Rows:
    def __init__(self, batch, seq, ctx_len, tile=ROW_TILE):
        self.batch, self.seq, self.ctx_len, self.tile = batch, seq, ctx_len, tile
        assert seq % tile == 0 and (batch * ctx_len) % tile == 0
        self.tiles_per_batch = seq // tile
        self.n_lat = batch * seq // tile
        self.n_ctx = batch * ctx_len // tile
        self.n_all = self.n_lat + self.n_ctx

    @classmethod
    def widest(cls, batch, seq, ctx_len, factor):
        tile = ROW_TILE * factor
        if seq % tile or (batch * ctx_len) % tile:
            tile = ROW_TILE
        return cls(batch, seq, ctx_len, tile)

    def mod_spec(self, layer, d):
        n_lat, tpb, batch = self.n_lat, self.tiles_per_batch, self.batch
        return pl.BlockSpec(
            (1, 1, N_MOD, d),
            lambda i: (layer, jnp.where(i < n_lat, i // tpb, batch), 0, 0))

    def rope_spec(self, width):
        n_lat, tpb = self.n_lat, self.tiles_per_batch
        return pl.BlockSpec((self.tile, width), lambda i: (jnp.where(i < n_lat, i % tpb, tpb), 0))

    def row_spec(self, width):
        return pl.BlockSpec((self.tile, width), lambda i: (i, 0))


def _mod_rows(mod_ref, base):
    return [mod_ref[0, 0, base + k:base + k + 1, :] for k in range(3)]


def _ffn_kernel(*refs, base, ffn_dim, chunks, final, n_lat_split, mixer):
    refs = list(refs)
    o_ref = refs.pop()
    fg_ref = refs.pop() if final else None
    x_ref = refs.pop(0)
    xc_ref = refs.pop(0) if n_lat_split is not None else None
    a_ref, wm_ref = (refs.pop(0), refs.pop(0)) if mixer else (None, None)
    mod_ref, g_ref, w1_ref, w2_ref = refs
    shift, scale, gate = _mod_rows(mod_ref, base)
    x = x_ref[...]
    if xc_ref is not None:
        x = jnp.where(pl.program_id(0) < n_lat_split, x, xc_ref[...])
    if mixer:
        x = x + mod_ref[0, 0, 5:6, :] * _dot(a_ref[...], wm_ref[...])
    h = _modulated(x, g_ref[...], shift, scale).astype(BF16)
    acc = None
    lo = 0
    for width in chunks:
        gt = _dot(h, w1_ref[:, lo:lo + width])
        up = _dot(h, w1_ref[:, ffn_dim + lo:ffn_dim + lo + width])
        a = (_silu(gt) * up).astype(BF16)
        part = _dot(a, w2_ref[lo:lo + width, :])
        acc = part if acc is None else acc + part
        lo += width
    y = x + (0.5 * gate) * acc
    if final:
        y = _rms_rows(y) * fg_ref[...]
    o_ref[...] = y


def _ffn_chunks(ffn_dim):
    assert ffn_dim % MXU_DEPTH == 0
    n_tiles = ffn_dim // MXU_DEPTH
    first = (n_tiles // FFN_SPLIT) * MXU_DEPTH
    sizes = [first] * (FFN_SPLIT - 1) + [ffn_dim - first * (FFN_SPLIT - 1)]
    return tuple(s for s in sizes if s)


def _ffn_call(rows, latent_only, x, mod, layer, base, gain, w1, w2, which, final_g=None,
              x_ctx=None, mixer=None):
    d = x.shape[1]
    ffn_dim = w2.shape[-2]
    chunks = _ffn_chunks(ffn_dim)
    final = final_g is not None
    n_lat = rows.n_lat
    n_tiles = n_lat if latent_only else rows.n_all
    if x_ctx is None:
        in_specs, args = [rows.row_spec(d)], [x]
    else:
        in_specs = [pl.BlockSpec((rows.tile, d), lambda i: (jnp.minimum(i, n_lat - 1), 0)),
                    pl.BlockSpec((rows.tile, d), lambda i: (jnp.maximum(i - n_lat, 0), 0))]
        args = [x, x_ctx]
    if mixer is not None:
        a, w_mix, mix_idx = mixer
        in_specs += [rows.row_spec(a.shape[1]), _resident(w_mix, (mix_idx,))]
        args += [a, w_mix]
    in_specs += [rows.mod_spec(layer, d), pl.BlockSpec((1, d), lambda i: (0, 0)),
                 _resident(w1, (layer, which)), _resident(w2, (layer, which))]
    args += [mod, gain.reshape(1, d), w1, w2]
    if final:
        in_specs.append(pl.BlockSpec((1, d), lambda i: (0, 0)))
        args.append(final_g.reshape(1, d))
    return pl.pallas_call(
        functools.partial(_ffn_kernel, base=base, ffn_dim=ffn_dim, chunks=chunks, final=final,
                          n_lat_split=None if x_ctx is None else n_lat,
                          mixer=mixer is not None),
        grid=(n_tiles,),
        in_specs=in_specs,
        out_specs=rows.row_spec(d),
        out_shape=jax.ShapeDtypeStruct((n_tiles * rows.tile, d), F32),
        compiler_params=_params(1),
        name="ffn",
    )(*args)


def _rope_halves(x1, x2, cos, sin):
    return x1 * cos - x2 * sin, x2 * cos + x1 * sin


def _proj_ret_kernel(x_ref, mod_ref, g_ref, w_ref, cos_ref, sin_ref, o_ref):
    shift, scale, _ = _mod_rows(mod_ref, 3)
    h = _modulated(x_ref[...], g_ref[...], shift, scale).astype(BF16)
    cos, sin = cos_ref[...], sin_ref[...]
    qd = RET_HEADS * RET_DK
    vd = RET_HEADS * RET_DV
    half = RET_DK // 2
    for part, mult in ((0, None), (1, RET_DK ** -0.5)):
        p = _dot(h, w_ref[:, part * qd:(part + 1) * qd])
        if mult is not None:
            p = p * mult
        for hh in range(RET_HEADS):
            lo = hh * RET_DK
            r1, r2 = _rope_halves(p[:, lo:lo + half], p[:, lo + half:lo + RET_DK], cos, sin)
            o_ref[:, part * qd + lo:part * qd + lo + half] = r1.astype(BF16)
            o_ref[:, part * qd + lo + half:part * qd + lo + RET_DK] = r2.astype(BF16)
    for part in range(2):
        lo = 2 * qd + part * vd
        o_ref[:, lo:lo + vd] = _dot(h, w_ref[:, lo:lo + vd]).astype(BF16)


def _proj_att_kernel(x_ref, mod_ref, g_ref, w_ref, wr_ref, qg_ref, kg_ref, cos_ref, sin_ref, o_ref):
    shift, scale, _ = _mod_rows(mod_ref, 3)
    qd = ATT_HEADS * ATT_HD
    kd = ATT_KV_HEADS * ATT_HD
    for r0 in range(0, o_ref.shape[0], ROW_TILE):
        rs = slice(r0, r0 + ROW_TILE)
        h = _modulated(x_ref[rs, :], g_ref[...], shift, scale).astype(BF16)
        cos, sin = cos_ref[rs, :], sin_ref[rs, :]

        def norm_rope(lo_col, n_heads, gains, gain_scale):
            p = _dot(h, w_ref[:, lo_col:lo_col + n_heads * ATT_HD])
            p_swapped = _dot(h, wr_ref[:, lo_col:lo_col + n_heads * ATT_HD])
            gain_cos = (gains[0:1, :] * gain_scale) * cos
            gain_sin = (gains[1:2, :] * gain_scale) * sin
            for hh in range(n_heads):
                lo = hh * ATT_HD
                ph = p[:, lo:lo + ATT_HD]
                inv = lax.rsqrt(jnp.mean(ph * ph, axis=-1, keepdims=True) + EPS)
                r = (ph * gain_cos + p_swapped[:, lo:lo + ATT_HD] * gain_sin) * inv
                o_ref[rs, lo_col + lo:lo_col + lo + ATT_HD] = r.astype(BF16)

        norm_rope(0, ATT_HEADS, qg_ref[...], ATT_EXP2_SCALE)
        norm_rope(qd, ATT_KV_HEADS, kg_ref[...], 1.0)
        o_ref[rs, qd + kd:] = _dot(h, w_ref[:, qd + kd:]).astype(BF16)


def _proj_call(kernel, name, rows, x, mod, layer, gain, w, which, extras, extra_specs):
    d = x.shape[1]
    n_out = w.shape[-1]
    return pl.pallas_call(
        kernel,
        grid=(rows.n_all,),
        in_specs=[rows.row_spec(d), rows.mod_spec(layer, d), pl.BlockSpec((1, d), lambda i: (0, 0)),
                  _resident(w, (which,))] + extra_specs,
        out_specs=rows.row_spec(n_out),
        out_shape=jax.ShapeDtypeStruct((rows.n_all * rows.tile, n_out), BF16),
        compiler_params=_params(1),
        name=name,
    )(x, mod, gain.reshape(1, d), w, *extras)


def _decay_tables(n, lg_f, lg_b):
    i = lax.broadcasted_iota(jnp.int32, (n, n), 0)
    j = lax.broadcasted_iota(jnp.int32, (n, n), 1)
    diff = (i - j).astype(F32)
    mask = (jnp.where(diff >= 0, jnp.exp(jnp.maximum(diff, 0.0) * lg_f), 0.0)
            + jnp.where(diff <= 0, jnp.exp(jnp.maximum(-diff, 0.0) * lg_b), 0.0))
    pos = lax.broadcasted_iota(jnp.int32, (n, 1), 0).astype(F32)
    xi_f = jnp.exp((pos + 1.0) * lg_f)
    xi_b = jnp.exp((n - pos) * lg_b)
    zeta_f = jnp.exp((n - 1.0 - pos) * lg_f)
    zeta_b = jnp.exp(pos * lg_b)
    return mask, xi_f, xi_b, zeta_f, zeta_b


def _scaled_bf16(x, col):
    return (x.astype(F32) * col).astype(BF16)


def _head_norm_gate(o, g):
    return (_rms_rows(o) * _silu(g.astype(F32))).astype(BF16)


def _scaled_pair(x, col_a, col_b):
    xf = x.astype(F32)
    return jnp.concatenate([(xf * col_a).astype(BF16), (xf * col_b).astype(BF16)], axis=1)


def _ret_kernel(ql_ref, kl_ref, vl_ref, gl_ref, qc_ref, kc_ref, vc_ref, gc_ref, dec_ref,
                ol_ref, oc_ref, st_ref, ub_ref, runf_ref, runb_ref, mask_ref, cmask_ref,
                col_ref, ccol_ref, *, n_chunks, chunk, ctx_len):
    dk = RET_DK
    lg_f = -jnp.exp(dec_ref[0, 0:1, 0:1])
    lg_b = -jnp.exp(dec_ref[0, 1:2, 0:1])

    @pl.when(pl.program_id(1) == 0)
    def _():
        mask, xi_f, xi_b, zeta_f, zeta_b = _decay_tables(chunk, lg_f, lg_b)
        mask_ref[...] = mask
        for k, col in enumerate((xi_f, xi_b, zeta_f, zeta_b)):
            col_ref[k] = col
        cmask, _, _, czeta_f, czeta_b = _decay_tables(ctx_len, lg_f, lg_b)
        cmask_ref[...] = cmask
        ccol_ref[0] = czeta_f
        ccol_ref[1] = czeta_b

    decay_f = jnp.exp(chunk * lg_f)
    decay_b = jnp.exp(chunk * lg_b)

    q, k, v = qc_ref[...], kc_ref[...], vc_ref[...]
    p = (_dot_nt(q, k) * cmask_ref[...]).astype(BF16)
    oc_ref[...] = _head_norm_gate(_dot(p, v), gc_ref[...])
    s0 = _dot_tn(_scaled_pair(k, ccol_ref[0], ccol_ref[1]), v)
    runf_ref[...] = s0[:dk]
    runb_ref[...] = s0[dk:]
    st_ref[0, :dk, :] = s0[:dk].astype(BF16)
    st_ref[n_chunks - 1, dk:, :] = s0[dk:].astype(BF16)

    def rows(n):
        return slice(n * chunk, (n + 1) * chunk)

    for n in range(n_chunks):
        u = _dot_tn(_scaled_pair(kl_ref[rows(n), :], col_ref[2], col_ref[3]), vl_ref[rows(n), :])
        if n + 1 < n_chunks:
            s = runf_ref[...] * decay_f + u[:dk]
            runf_ref[...] = s
            st_ref[n + 1, :dk, :] = s.astype(BF16)
        if n > 0:
            ub_ref[n] = u[dk:]

    for n in range(n_chunks - 1, 0, -1):
        s = runb_ref[...] * decay_b + ub_ref[n]
        runb_ref[...] = s
        st_ref[n - 1, dk:, :] = s.astype(BF16)

    for n in range(n_chunks):
        r = rows(n)
        q, k, v = ql_ref[r, :], kl_ref[r, :], vl_ref[r, :]
        p = (_dot_nt(q, k) * mask_ref[...]).astype(BF16)
        o = _dot(p, v) + _dot(_scaled_pair(q, col_ref[0], col_ref[1]), st_ref[n])
        ol_ref[r, :] = _head_norm_gate(o, gl_ref[r, :])


def _ret_call(batch, seq, ctx_len, proj, dec_tab):
    chunk = min(RET_CHUNK, seq)
    assert seq % chunk == 0
    n_chunks = seq // chunk
    ctx_row0 = batch * seq // ctx_len
    kq = RET_HEADS
    kv = 2 * RET_HEADS * RET_DK // RET_DV
    kg = kv + RET_HEADS
    vd = RET_HEADS * RET_DV
    in_specs = [
        pl.BlockSpec((seq, RET_DK), lambda h, b: (b, h)),
        pl.BlockSpec((seq, RET_DK), lambda h, b: (b, kq + h)),
        pl.BlockSpec((seq, RET_DV), lambda h, b: (b, kv + h)),
        pl.BlockSpec((seq, RET_DV), lambda h, b: (b, kg + h)),
        pl.BlockSpec((ctx_len, RET_DK), lambda h, b: (ctx_row0 + b, h)),
        pl.BlockSpec((ctx_len, RET_DK), lambda h, b: (ctx_row0 + b, kq + h)),
        pl.BlockSpec((ctx_len, RET_DV), lambda h, b: (ctx_row0 + b, kv + h)),
        pl.BlockSpec((ctx_len, RET_DV), lambda h, b: (ctx_row0 + b, kg + h)),
        pl.BlockSpec((1, 8, 128), lambda h, b: (h, 0, 0)),
    ]
    return pl.pallas_call(
        functools.partial(_ret_kernel, n_chunks=n_chunks, chunk=chunk, ctx_len=ctx_len),
        grid=(RET_HEADS, batch),
        in_specs=in_specs,
        out_specs=[pl.BlockSpec((seq, RET_DV), lambda h, b: (b, h)),
                   pl.BlockSpec((ctx_len, RET_DV), lambda h, b: (b, h))],
        out_shape=[jax.ShapeDtypeStruct((batch * seq, vd), BF16),
                   jax.ShapeDtypeStruct((batch * ctx_len, vd), BF16)],
        scratch_shapes=[pltpu.VMEM((n_chunks, 2 * RET_DK, RET_DV), BF16),
                        pltpu.VMEM((n_chunks, RET_DK, RET_DV), F32),
                        pltpu.VMEM((RET_DK, RET_DV), F32),
                        pltpu.VMEM((RET_DK, RET_DV), F32),
                        pltpu.VMEM((chunk, chunk), F32),
                        pltpu.VMEM((ctx_len, ctx_len), F32),
                        pltpu.VMEM((4, chunk, 1), F32),
                        pltpu.VMEM((2, ctx_len, 1), F32)],
        compiler_params=_params(2),
        name="ret",
    )(proj, proj, proj, proj, proj, proj, proj, proj, dec_tab)


def _att_kernel(q_ref, kl_ref, vl_ref, kc_ref, vc_ref, o_ref, s_ref, p_ref, va_ref, *, n_q, seq):
    qi = pl.program_id(2)
    group = ATT_HEADS // ATT_KV_HEADS
    tq = q_ref.shape[0]
    n_keys = s_ref.shape[1]

    @pl.when(qi == 0)
    def _():
        va_ref[:seq, :ATT_HD] = vl_ref[...]
        va_ref[seq:, :ATT_HD] = vc_ref[...]
        va_ref[:, ATT_HD:] = jnp.ones((n_keys, ATT_HD), BF16)

    def attend(col_lo):
        slabs = range(col_lo, n_keys, LANES)
        for g in range(group):
            q = q_ref[:, g * ATT_HD:(g + 1) * ATT_HD]
            if col_lo < seq:
                s_ref[g * tq:(g + 1) * tq, :seq] = _dot_nt(q, kl_ref[...])
            s_ref[g * tq:(g + 1) * tq, seq:] = _dot_nt(q, kc_ref[...])
        for g in range(group):
            rs = slice(g * tq, (g + 1) * tq)
            m = None
            for c0 in slabs:
                t = s_ref[rs, c0:c0 + LANES]
                m = t if m is None else jnp.maximum(m, t)
            m = jnp.max(m, axis=-1, keepdims=True)
            o = None
            for k0 in range(col_lo, n_keys, ATT_KEY_CHUNK):
                k1 = min(k0 + ATT_KEY_CHUNK, seq if k0 < seq else n_keys)
                for c0 in range(k0, k1, LANES):
                    e = jnp.exp2(s_ref[rs, c0:c0 + LANES] - m)
                    p_ref[rs, c0:c0 + LANES] = e.astype(BF16)
                part = _dot(p_ref[rs, k0:k1], va_ref[k0:k1, :])
                o = part if o is None else o + part
            o_ref[:, g * ATT_HD:(g + 1) * ATT_HD] = (o[:, :ATT_HD] / o[:, ATT_HD:]).astype(BF16)

    @pl.when(qi < n_q)
    def _():
        attend(0)

    @pl.when(qi >= n_q)
    def _():
        attend(seq)


def _att_call(batch, seq, ctx_len, proj, ctx_out):
    tq = ATT_Q_TILE
    assert ctx_len == tq and seq % tq == 0
    n_q = seq // tq
    ctx_row0 = batch * seq // ctx_len
    group_w = (ATT_HEADS // ATT_KV_HEADS) * ATT_HD
    k_col = ATT_HEADS
    v_col = ATT_HEADS + ATT_KV_HEADS
    n_rows = batch * seq + (batch * ctx_len if ctx_out else 0)

    def q_map(b, kh, qi):
        return (jnp.where(qi < n_q, b * n_q + qi, ctx_row0 + b), kh)

    return pl.pallas_call(
        functools.partial(_att_kernel, n_q=n_q, seq=seq),
        grid=(batch, ATT_KV_HEADS, n_q + (1 if ctx_out else 0)),
        scratch_shapes=[pltpu.VMEM((group_w // ATT_HD * tq, seq + ctx_len), F32),
                        pltpu.VMEM((group_w // ATT_HD * tq, seq + ctx_len), BF16),
                        pltpu.VMEM((seq + ctx_len, 2 * ATT_HD), BF16)],
        in_specs=[
            pl.BlockSpec((tq, group_w), q_map),
            pl.BlockSpec((seq, ATT_HD), lambda b, kh, qi: (b, k_col + kh)),
            pl.BlockSpec((seq, ATT_HD), lambda b, kh, qi: (b, v_col + kh)),
            pl.BlockSpec((ctx_len, ATT_HD), lambda b, kh, qi: (ctx_row0 + b, k_col + kh)),
            pl.BlockSpec((ctx_len, ATT_HD), lambda b, kh, qi: (ctx_row0 + b, v_col + kh)),
        ],
        out_specs=pl.BlockSpec((tq, group_w), q_map),
        out_shape=jax.ShapeDtypeStruct((n_rows, ATT_HEADS * ATT_HD), BF16),
        compiler_params=_params(3),
        name="att",
    )(proj, proj, proj, proj, proj)


def _oproj_kernel(x_ref, mod_ref, w_ref, *rest, n_lat):
    gate = mod_ref[0, 0, 5:6, :]
    if len(rest) == 2:
        a_ref, o_ref = rest
        o_ref[...] = x_ref[...] + gate * _dot(a_ref[...], w_ref[...])
        return
    al_ref, ac_ref, o_ref = rest
    i = pl.program_id(0)

    @pl.when(i < n_lat)
    def _():
        o_ref[...] = x_ref[...] + gate * _dot(al_ref[...], w_ref[...])

    @pl.when(i >= n_lat)
    def _():
        o_ref[...] = x_ref[...] + gate * _dot(ac_ref[...], w_ref[...])


def _oproj_call(rows, latent_only, x, mod, layer, w, which, a_lat, a_ctx=None):
    d = x.shape[1]
    k = w.shape[-2]
    n_lat = rows.n_lat
    n_tiles = n_lat if latent_only else rows.n_all
    in_specs = [rows.row_spec(d), rows.mod_spec(layer, d), _resident(w, (which,))]
    args = [x, mod, w]
    if a_ctx is None:
        in_specs.append(rows.row_spec(k))
        args.append(a_lat)
    else:
        in_specs.append(pl.BlockSpec((rows.tile, k), lambda i: (jnp.minimum(i, n_lat - 1), 0)))
        in_specs.append(pl.BlockSpec((rows.tile, k), lambda i: (jnp.maximum(i - n_lat, 0), 0)))
        args += [a_lat, a_ctx]
    return pl.pallas_call(
        functools.partial(_oproj_kernel, n_lat=n_lat),
        grid=(n_tiles,),
        in_specs=in_specs,
        out_specs=rows.row_spec(d),
        out_shape=jax.ShapeDtypeStruct((n_tiles * rows.tile, d), F32),
        compiler_params=_params(1),
        name="oproj",
    )(*args)


def _with_identity_rows(cos, sin, pad_rows):
    pad = (pad_rows, cos.shape[1])
    return (jnp.concatenate([cos, jnp.ones(pad, F32)], axis=0),
            jnp.concatenate([sin, jnp.zeros(pad, F32)], axis=0))


def _seq_rope_tables(seq, pad_rows):
    half = RET_DK // 2
    freqs = ROPE_THETA ** (-jnp.arange(half, dtype=F32) / half)
    ang = jnp.arange(seq, dtype=F32)[:, None] * freqs
    return _with_identity_rows(jnp.cos(ang), jnp.sin(ang), pad_rows)


def _axial_rope_tables(seq, pad_rows):
    quarter = ATT_HD // 4
    tok = jnp.arange(seq)
    r = (tok // GRID_W).astype(F32)
    cl = (tok % GRID_W).astype(F32)
    freqs = ROPE_THETA ** (-jnp.arange(quarter, dtype=F32) / quarter)
    ang = jnp.concatenate([r[:, None] * freqs, cl[:, None] * freqs], axis=-1)
    cos, sin = jnp.cos(ang), jnp.sin(ang)
    return _with_identity_rows(jnp.concatenate([cos, cos], axis=-1),
                               jnp.concatenate([-sin, sin], axis=-1), pad_rows)


def _swap_head_halves(a):
    lead = a.shape[:-1]
    return a.reshape(lead + (-1, 2, ATT_HD // 2))[..., ::-1, :].reshape(a.shape)


def _with_swapped_halves(gain):
    return jnp.stack([gain, _swap_head_halves(gain)], axis=0)


def kernel(x, c, ctx, c_ctx, ada_w, ada_b, norm_g, ffn_w1, ffn_w2, ret_w_in, ret_w_out,
           ret_decay_f, ret_decay_b, att_w_qkv, att_w_o, att_q_gain, att_k_gain, final_g):
    batch, seq, d = x.shape
    ctx_len = ctx.shape[1]
    depth = ada_w.shape[0]
    rows = _Rows(batch, seq, ctx_len)
    rows_wide = _Rows.widest(batch, seq, ctx_len, WIDE_TILE_FACTOR)

    cc = jnp.concatenate(
        [c, c_ctx[None], jnp.zeros((ADA_ROWS - batch - 1, d), F32)], axis=0)
    mod = _ada_call(cc, ada_w, ada_b).reshape(depth, ADA_ROWS, N_MOD, d)

    w1 = ffn_w1.astype(BF16)
    w2 = ffn_w2.astype(BF16)
    w_in = ret_w_in.astype(BF16)
    w_out = ret_w_out.astype(BF16)
    w_qkv = att_w_qkv.astype(BF16)
    w_o = att_w_o.astype(BF16)

    rows_att = _Rows.widest(batch, seq, ctx_len, PROJ_ATT_SUBTILES)
    ret_cos, ret_sin = _seq_rope_tables(seq, rows.tile)
    att_cos, att_sin = _axial_rope_tables(seq, rows_att.tile)
    ret_rope_specs = [rows.rope_spec(RET_DK // 2)] * 2
    att_rope_specs = [rows_att.rope_spec(ATT_HD)] * 2
    gain_spec = pl.BlockSpec((2, ATT_HD), lambda i: (0, 0))
    w_qk_swapped = _swap_head_halves(w_qkv[..., :(ATT_HEADS + ATT_KV_HEADS) * ATT_HD])

    xs = x.reshape(batch * seq, d)
    xs_ctx = ctx.reshape(batch * ctx_len, d)

    for i in range(depth):
        last = i == depth - 1
        j = i // 2
        xs = _ffn_call(rows_wide, False, xs, mod, i, 0, norm_g[i, 0], w1, w2, 0,
                       x_ctx=xs_ctx if i == 0 else None)
        if i % 2 == 0:
            proj = _proj_call(_proj_ret_kernel, "proj_ret", rows, xs, mod, i, norm_g[i, 1],
                              w_in, j, [ret_cos, ret_sin], ret_rope_specs)
            dec_tab = jnp.broadcast_to(
                jnp.stack([ret_decay_f[j], ret_decay_b[j]], axis=1)[:, :, None],
                (RET_HEADS, 2, 128))
            dec_tab = jnp.concatenate([dec_tab, jnp.zeros((RET_HEADS, 6, 128), F32)], axis=1)
            a_lat, a_ctx = _ret_call(batch, seq, ctx_len, proj, dec_tab)
            xs = _oproj_call(rows_wide, last, xs, mod, i, w_out, j, a_lat,
                             None if last else a_ctx)
            mixer = None
        else:
            proj = _proj_call(_proj_att_kernel, "proj_att", rows_att, xs, mod, i, norm_g[i, 1],
                              w_qkv, j,
                              [w_qk_swapped, _with_swapped_halves(att_q_gain[j]),
                               _with_swapped_halves(att_k_gain[j]), att_cos, att_sin],
                              [_resident(w_qk_swapped, (j,)), gain_spec, gain_spec]
                              + att_rope_specs)
            mixer = (_att_call(batch, seq, ctx_len, proj, not last), w_o, j)
        xs = _ffn_call(rows_wide, last, xs, mod, i, 6, norm_g[i, 2], w1, w2, 1,
                       final_g if last else None, mixer=mixer)
    return xs.reshape(batch, seq, d)
```

```python
import functools

import jax
import jax.numpy as jnp
from jax import lax
from jax.experimental import pallas as pl
from jax.experimental.pallas import tpu as pltpu

F32 = jnp.float32
BF16 = jnp.bfloat16

N_MOD = 9
RET_HEADS = 4
RET_DK = 256
RET_DV = 512
ATT_HEADS = 8
ATT_KV_HEADS = 2
ATT_HD = 128
GRID_W = 64
ROPE_THETA = 10000.0
EPS = 1e-6

ROW_TILE = 512
RET_CHUNK = 256
ATT_Q_TILE = 256
ATT_KEY_CHUNK = 512
ATT_EXP2_SCALE = ATT_HD ** -0.5 * 1.4426950408889634
LANES = 128
MXU_DEPTH = 256
FFN_SPLIT = 3
WIDE_TILE_FACTOR = 2
PROJ_ATT_SUBTILES = 2
ADA_ROWS = 16
VMEM_LIMIT = 58 * 1024 * 1024


def _params(n_axes):
    return pltpu.CompilerParams(
        dimension_semantics=("arbitrary",) * n_axes, vmem_limit_bytes=VMEM_LIMIT)


def _resident(stacked, lead):
    tail = stacked.shape[len(lead):]
    index = tuple(lead) + (0,) * len(tail)
    return pl.BlockSpec((None,) * len(lead) + tail, lambda *_: index,
                        pipeline_mode=pl.Buffered(1))


def _silu(x):
    return x * jax.nn.sigmoid(x)


def _rms_rows(x):
    return x * lax.rsqrt(jnp.mean(x * x, axis=-1, keepdims=True) + EPS)


def _modulated(x, gain, shift, scale):
    return (_rms_rows(x) * gain) * (1.0 + scale) + shift


def _dot(a, b):
    return jnp.dot(a, b, preferred_element_type=F32)


def _dot_nt(a, b):
    return lax.dot_general(a, b, (((1,), (1,)), ((), ())), preferred_element_type=F32)


def _dot_tn(a, b):
    return lax.dot_general(a, b, (((0,), (0,)), ((), ())), preferred_element_type=F32)


def _ada_kernel(cc_ref, w_ref, b_ref, o_ref):
    s = _silu(cc_ref[...])
    o_ref[0] = _dot(s.astype(BF16), w_ref[0].astype(BF16)) + b_ref[0]


def _ada_call(cc, ada_w, ada_b):
    depth, d, n = ada_w.shape
    tn = 1024
    return pl.pallas_call(
        _ada_kernel,
        grid=(depth, n // tn),
        in_specs=[
            pl.BlockSpec((ADA_ROWS, d), lambda l, j: (0, 0)),
            pl.BlockSpec((1, d, tn), lambda l, j: (l, 0, j)),
            pl.BlockSpec((1, 1, tn), lambda l, j: (l, 0, j)),
        ],
        out_specs=pl.BlockSpec((1, ADA_ROWS, tn), lambda l, j: (l, 0, j)),
        out_shape=jax.ShapeDtypeStruct((depth, ADA_ROWS, n), F32),
        compiler_params=_params(2),
        name="ada",
    )(cc, ada_w, ada_b.reshape(depth, 1, n))


class _Rows:
    def __init__(self, batch, seq, ctx_len, tile=ROW_TILE):
        self.batch, self.seq, self.ctx_len, self.tile = batch, seq, ctx_len, tile
        assert seq % tile == 0 and (batch * ctx_len) % tile == 0
        self.tiles_per_batch = seq // tile
        self.n_lat = batch * seq // tile
        self.n_ctx = batch * ctx_len // tile
        self.n_all = self.n_lat + self.n_ctx

    @classmethod
    def widest(cls, batch, seq, ctx_len, factor):
        tile = ROW_TILE * factor
        if seq % tile or (batch * ctx_len) % tile:
            tile = ROW_TILE
        return cls(batch, seq, ctx_len, tile)

    def mod_spec(self, layer, d):
        n_lat, tpb, batch = self.n_lat, self.tiles_per_batch, self.batch
        return pl.BlockSpec(
            (1, 1, N_MOD, d),
            lambda i: (layer, jnp.where(i < n_lat, i // tpb, batch), 0, 0))

    def rope_spec(self, width):
        n_lat, tpb = self.n_lat, self.tiles_per_batch
        return pl.BlockSpec((self.tile, width), lambda i: (jnp.where(i < n_lat, i % tpb, tpb), 0))

    def row_spec(self, width):
        return pl.BlockSpec((self.tile, width), lambda i: (i, 0))


def _mod_rows(mod_ref, base):
    return [mod_ref[0, 0, base + k:base + k + 1, :] for k in range(3)]


def _ffn_kernel(*refs, base, ffn_dim, chunks, final, n_lat_split, mixer):
    refs = list(refs)
    o_ref = refs.pop()
    fg_ref = refs.pop() if final else None
    x_ref = refs.pop(0)
    xc_ref = refs.pop(0) if n_lat_split is not None else None
    a_ref, wm_ref = (refs.pop(0), refs.pop(0)) if mixer else (None, None)
    mod_ref, g_ref, w1_ref, w2_ref = refs
    shift, scale, gate = _mod_rows(mod_ref, base)
    x = x_ref[...]
    if xc_ref is not None:
        x = jnp.where(pl.program_id(0) < n_lat_split, x, xc_ref[...])
    if mixer:
        x = x + mod_ref[0, 0, 5:6, :] * _dot(a_ref[...], wm_ref[...])
    h = _modulated(x, g_ref[...], shift, scale).astype(BF16)
    acc = None
    lo = 0
    for width in chunks:
        gt = _dot(h, w1_ref[:, lo:lo + width])
        up = _dot(h, w1_ref[:, ffn_dim + lo:ffn_dim + lo + width])
        a = (_silu(gt) * up).astype(BF16)
        part = _dot(a, w2_ref[lo:lo + width, :])
        acc = part if acc is None else acc + part
        lo += width
    y = x + (0.5 * gate) * acc
    if final:
        y = _rms_rows(y) * fg_ref[...]
    o_ref[...] = y


def _ffn_chunks(ffn_dim):
    assert ffn_dim % MXU_DEPTH == 0
    n_tiles = ffn_dim // MXU_DEPTH
    first = (n_tiles // FFN_SPLIT) * MXU_DEPTH
    sizes = [first] * (FFN_SPLIT - 1) + [ffn_dim - first * (FFN_SPLIT - 1)]
    return tuple(s for s in sizes if s)


def _ffn_call(rows, latent_only, x, mod, layer, base, gain, w1, w2, which, final_g=None,
              x_ctx=None, mixer=None):
    d = x.shape[1]
    ffn_dim = w2.shape[-2]
    chunks = _ffn_chunks(ffn_dim)
    final = final_g is not None
    n_lat = rows.n_lat
    n_tiles = n_lat if latent_only else rows.n_all
    if x_ctx is None:
        in_specs, args = [rows.row_spec(d)], [x]
    else:
        in_specs = [pl.BlockSpec((rows.tile, d), lambda i: (jnp.minimum(i, n_lat - 1), 0)),
                    pl.BlockSpec((rows.tile, d), lambda i: (jnp.maximum(i - n_lat, 0), 0))]
        args = [x, x_ctx]
    if mixer is not None:
        a, w_mix, mix_idx = mixer
        in_specs += [rows.row_spec(a.shape[1]), _resident(w_mix, (mix_idx,))]
        args += [a, w_mix]
    in_specs += [rows.mod_spec(layer, d), pl.BlockSpec((1, d), lambda i: (0, 0)),
                 _resident(w1, (layer, which)), _resident(w2, (layer, which))]
    args += [mod, gain.reshape(1, d), w1, w2]
    if final:
        in_specs.append(pl.BlockSpec((1, d), lambda i: (0, 0)))
        args.append(final_g.reshape(1, d))
    return pl.pallas_call(
        functools.partial(_ffn_kernel, base=base, ffn_dim=ffn_dim, chunks=chunks, final=final,
                          n_lat_split=None if x_ctx is None else n_lat,
                          mixer=mixer is not None),
        grid=(n_tiles,),
        in_specs=in_specs,
        out_specs=rows.row_spec(d),
        out_shape=jax.ShapeDtypeStruct((n_tiles * rows.tile, d), F32),
        compiler_params=_params(1),
        name="ffn",
    )(*args)


def _rope_halves(x1, x2, cos, sin):
    return x1 * cos - x2 * sin, x2 * cos + x1 * sin


def _proj_ret_kernel(x_ref, mod_ref, g_ref, w_ref, cos_ref, sin_ref, o_ref):
    shift, scale, _ = _mod_rows(mod_ref, 3)
    qd = RET_HEADS * RET_DK
    vd = RET_HEADS * RET_DV
    half = RET_DK // 2
    v_lo, g_lo = 2 * qd, 2 * qd + vd
    for r0 in range(0, o_ref.shape[0], ROW_TILE):
        rs = slice(r0, r0 + ROW_TILE)
        h = _modulated(x_ref[rs, :], g_ref[...], shift, scale).astype(BF16)
        cos, sin = cos_ref[rs, :], sin_ref[rs, :]
        o_ref[rs, g_lo:g_lo + vd] = _silu(_dot(h, w_ref[:, g_lo:g_lo + vd])).astype(BF16)
        for part, mult in ((0, None), (1, RET_DK ** -0.5)):
            p = _dot(h, w_ref[:, part * qd:(part + 1) * qd])
            if mult is not None:
                p = p * mult
            for hh in range(RET_HEADS):
                lo = hh * RET_DK
                r1, r2 = _rope_halves(p[:, lo:lo + half], p[:, lo + half:lo + RET_DK], cos, sin)
                o_ref[rs, part * qd + lo:part * qd + lo + half] = r1.astype(BF16)
                o_ref[rs, part * qd + lo + half:part * qd + lo + RET_DK] = r2.astype(BF16)
        o_ref[rs, v_lo:g_lo] = _dot(h, w_ref[:, v_lo:g_lo]).astype(BF16)


def _proj_att_kernel(x_ref, mod_ref, g_ref, w_ref, wr_ref, qg_ref, kg_ref, cos_ref, sin_ref, o_ref):
    shift, scale, _ = _mod_rows(mod_ref, 3)
    qd = ATT_HEADS * ATT_HD
    kd = ATT_KV_HEADS * ATT_HD
    for r0 in range(0, o_ref.shape[0], ROW_TILE):
        rs = slice(r0, r0 + ROW_TILE)
        h = _modulated(x_ref[rs, :], g_ref[...], shift, scale).astype(BF16)
        cos, sin = cos_ref[rs, :], sin_ref[rs, :]

        def norm_rope(lo_col, n_heads, gains, gain_scale):
            p = _dot(h, w_ref[:, lo_col:lo_col + n_heads * ATT_HD])
            p_swapped = _dot(h, wr_ref[:, lo_col:lo_col + n_heads * ATT_HD])
            gain_cos = (gains[0:1, :] * gain_scale) * cos
            gain_sin = (gains[1:2, :] * gain_scale) * sin
            for hh in range(n_heads):
                lo = hh * ATT_HD
                ph = p[:, lo:lo + ATT_HD]
                inv = lax.rsqrt(jnp.mean(ph * ph, axis=-1, keepdims=True) + EPS)
                r = (ph * gain_cos + p_swapped[:, lo:lo + ATT_HD] * gain_sin) * inv
                o_ref[rs, lo_col + lo:lo_col + lo + ATT_HD] = r.astype(BF16)

        norm_rope(0, ATT_HEADS, qg_ref[...], ATT_EXP2_SCALE)
        norm_rope(qd, ATT_KV_HEADS, kg_ref[...], 1.0)
        o_ref[rs, qd + kd:] = _dot(h, w_ref[:, qd + kd:]).astype(BF16)


def _proj_call(kernel, name, rows, x, mod, layer, gain, w, which, extras, extra_specs):
    d = x.shape[1]
    n_out = w.shape[-1]
    return pl.pallas_call(
        kernel,
        grid=(rows.n_all,),
        in_specs=[rows.row_spec(d), rows.mod_spec(layer, d), pl.BlockSpec((1, d), lambda i: (0, 0)),
                  _resident(w, (which,))] + extra_specs,
        out_specs=rows.row_spec(n_out),
        out_shape=jax.ShapeDtypeStruct((rows.n_all * rows.tile, n_out), BF16),
        compiler_params=_params(1),
        name=name,
    )(x, mod, gain.reshape(1, d), w, *extras)


def _decay_tables(n, lg_f, lg_b):
    i = lax.broadcasted_iota(jnp.int32, (n, n), 0)
    j = lax.broadcasted_iota(jnp.int32, (n, n), 1)
    diff = (i - j).astype(F32)
    mask = (jnp.where(diff >= 0, jnp.exp(jnp.maximum(diff, 0.0) * lg_f), 0.0)
            + jnp.where(diff <= 0, jnp.exp(jnp.maximum(-diff, 0.0) * lg_b), 0.0))
    pos = lax.broadcasted_iota(jnp.int32, (n, 1), 0).astype(F32)
    xi_f = jnp.exp((pos + 1.0) * lg_f)
    xi_b = jnp.exp((n - pos) * lg_b)
    zeta_f = jnp.exp((n - 1.0 - pos) * lg_f)
    zeta_b = jnp.exp(pos * lg_b)
    return mask, xi_f, xi_b, zeta_f, zeta_b


def _scaled_bf16(x, col):
    return (x.astype(F32) * col).astype(BF16)


def _head_norm_gate(o, silu_g):
    return (_rms_rows(o) * silu_g.astype(F32)).astype(BF16)


def _scaled_pair(x, col_a, col_b):
    xf = x.astype(F32)
    return jnp.concatenate([(xf * col_a).astype(BF16), (xf * col_b).astype(BF16)], axis=1)


def _ret_kernel(ql_ref, kl_ref, vl_ref, gl_ref, qc_ref, kc_ref, vc_ref, gc_ref, dec_ref,
                ol_ref, oc_ref, st_ref, ub_ref, runf_ref, runb_ref, mask_ref, cmask_ref,
                col_ref, ccol_ref, *, n_chunks, chunk, ctx_len):
    dk = RET_DK
    lg_f = -jnp.exp(dec_ref[0, 0:1, 0:1])
    lg_b = -jnp.exp(dec_ref[0, 1:2, 0:1])

    @pl.when(pl.program_id(1) == 0)
    def _():
        mask, xi_f, xi_b, zeta_f, zeta_b = _decay_tables(chunk, lg_f, lg_b)
        mask_ref[...] = mask
        for k, col in enumerate((xi_f, xi_b, zeta_f, zeta_b)):
            col_ref[k] = col
        cmask, _, _, czeta_f, czeta_b = _decay_tables(ctx_len, lg_f, lg_b)
        cmask_ref[...] = cmask
        ccol_ref[0] = czeta_f
        ccol_ref[1] = czeta_b

    decay_f = jnp.exp(chunk * lg_f)
    decay_b = jnp.exp(chunk * lg_b)

    q, k, v = qc_ref[...], kc_ref[...], vc_ref[...]
    p = (_dot_nt(q, k) * cmask_ref[...]).astype(BF16)
    oc_ref[...] = _head_norm_gate(_dot(p, v), gc_ref[...])
    s0 = _dot_tn(_scaled_pair(k, ccol_ref[0], ccol_ref[1]), v)
    runf_ref[...] = s0[:dk]
    runb_ref[...] = s0[dk:]
    st_ref[0, :dk, :] = s0[:dk].astype(BF16)
    st_ref[n_chunks - 1, dk:, :] = s0[dk:].astype(BF16)

    def rows(n):
        return slice(n * chunk, (n + 1) * chunk)

    for n in range(n_chunks):
        u = _dot_tn(_scaled_pair(kl_ref[rows(n), :], col_ref[2], col_ref[3]), vl_ref[rows(n), :])
        if n + 1 < n_chunks:
            s = runf_ref[...] * decay_f + u[:dk]
            runf_ref[...] = s
            st_ref[n + 1, :dk, :] = s.astype(BF16)
        if n > 0:
            ub_ref[n] = u[dk:]

    for n in range(n_chunks - 1, 0, -1):
        s = runb_ref[...] * decay_b + ub_ref[n]
        runb_ref[...] = s
        st_ref[n - 1, dk:, :] = s.astype(BF16)

    for n in range(n_chunks):
        r = rows(n)
        q, k, v = ql_ref[r, :], kl_ref[r, :], vl_ref[r, :]
        p = (_dot_nt(q, k) * mask_ref[...]).astype(BF16)
        o = _dot(p, v) + _dot(_scaled_pair(q, col_ref[0], col_ref[1]), st_ref[n])
        ol_ref[r, :] = _head_norm_gate(o, gl_ref[r, :])


def _ret_call(batch, seq, ctx_len, proj, dec_tab):
    chunk = min(RET_CHUNK, seq)
    assert seq % chunk == 0
    n_chunks = seq // chunk
    ctx_row0 = batch * seq // ctx_len
    kq = RET_HEADS
    kv = 2 * RET_HEADS * RET_DK // RET_DV
    kg = kv + RET_HEADS
    vd = RET_HEADS * RET_DV
    in_specs = [
        pl.BlockSpec((seq, RET_DK), lambda h, b: (b, h)),
        pl.BlockSpec((seq, RET_DK), lambda h, b: (b, kq + h)),
        pl.BlockSpec((seq, RET_DV), lambda h, b: (b, kv + h)),
        pl.BlockSpec((seq, RET_DV), lambda h, b: (b, kg + h)),
        pl.BlockSpec((ctx_len, RET_DK), lambda h, b: (ctx_row0 + b, h)),
        pl.BlockSpec((ctx_len, RET_DK), lambda h, b: (ctx_row0 + b, kq + h)),
        pl.BlockSpec((ctx_len, RET_DV), lambda h, b: (ctx_row0 + b, kv + h)),
        pl.BlockSpec((ctx_len, RET_DV), lambda h, b: (ctx_row0 + b, kg + h)),
        pl.BlockSpec((1, 8, 128), lambda h, b: (h, 0, 0)),
    ]
    return pl.pallas_call(
        functools.partial(_ret_kernel, n_chunks=n_chunks, chunk=chunk, ctx_len=ctx_len),
        grid=(RET_HEADS, batch),
        in_specs=in_specs,
        out_specs=[pl.BlockSpec((seq, RET_DV), lambda h, b: (b, h)),
                   pl.BlockSpec((ctx_len, RET_DV), lambda h, b: (b, h))],
        out_shape=[jax.ShapeDtypeStruct((batch * seq, vd), BF16),
                   jax.ShapeDtypeStruct((batch * ctx_len, vd), BF16)],
        scratch_shapes=[pltpu.VMEM((n_chunks, 2 * RET_DK, RET_DV), BF16),
                        pltpu.VMEM((n_chunks, RET_DK, RET_DV), F32),
                        pltpu.VMEM((RET_DK, RET_DV), F32),
                        pltpu.VMEM((RET_DK, RET_DV), F32),
                        pltpu.VMEM((chunk, chunk), F32),
                        pltpu.VMEM((ctx_len, ctx_len), F32),
                        pltpu.VMEM((4, chunk, 1), F32),
                        pltpu.VMEM((2, ctx_len, 1), F32)],
        compiler_params=_params(2),
        name="ret",
    )(proj, proj, proj, proj, proj, proj, proj, proj, dec_tab)


def _att_kernel(q_ref, kl_ref, vl_ref, kc_ref, vc_ref, o_ref, s_ref, p_ref, va_ref, *, n_q, seq):
    qi = pl.program_id(2)
    group = ATT_HEADS // ATT_KV_HEADS
    tq = q_ref.shape[0]
    n_keys = s_ref.shape[1]

    @pl.when(qi == 0)
    def _():
        va_ref[:seq, :ATT_HD] = vl_ref[...]
        va_ref[seq:, :ATT_HD] = vc_ref[...]
        va_ref[:, ATT_HD:] = jnp.ones((n_keys, ATT_HD), BF16)

    def attend(col_lo):
        slabs = range(col_lo, n_keys, LANES)
        for g in range(group):
            q = q_ref[:, g * ATT_HD:(g + 1) * ATT_HD]
            if col_lo < seq:
                s_ref[g * tq:(g + 1) * tq, :seq] = _dot_nt(q, kl_ref[...])
            s_ref[g * tq:(g + 1) * tq, seq:] = _dot_nt(q, kc_ref[...])
        for g in range(group):
            rs = slice(g * tq, (g + 1) * tq)
            m = None
            for c0 in slabs:
                t = s_ref[rs, c0:c0 + LANES]
                m = t if m is None else jnp.maximum(m, t)
            m = jnp.max(m, axis=-1, keepdims=True)
            o = None
            for k0 in range(col_lo, n_keys, ATT_KEY_CHUNK):
                k1 = min(k0 + ATT_KEY_CHUNK, seq if k0 < seq else n_keys)
                for c0 in range(k0, k1, LANES):
                    e = jnp.exp2(s_ref[rs, c0:c0 + LANES] - m)
                    p_ref[rs, c0:c0 + LANES] = e.astype(BF16)
                part = _dot(p_ref[rs, k0:k1], va_ref[k0:k1, :])
                o = part if o is None else o + part
            o_ref[:, g * ATT_HD:(g + 1) * ATT_HD] = (o[:, :ATT_HD] / o[:, ATT_HD:]).astype(BF16)

    @pl.when(qi < n_q)
    def _():
        attend(0)

    @pl.when(qi >= n_q)
    def _():
        attend(seq)


def _att_call(batch, seq, ctx_len, proj, ctx_out):
    tq = ATT_Q_TILE
    assert ctx_len == tq and seq % tq == 0
    n_q = seq // tq
    ctx_row0 = batch * seq // ctx_len
    group_w = (ATT_HEADS // ATT_KV_HEADS) * ATT_HD
    k_col = ATT_HEADS
    v_col = ATT_HEADS + ATT_KV_HEADS
    n_rows = batch * seq + (batch * ctx_len if ctx_out else 0)

    def q_map(b, kh, qi):
        return (jnp.where(qi < n_q, b * n_q + qi, ctx_row0 + b), kh)

    return pl.pallas_call(
        functools.partial(_att_kernel, n_q=n_q, seq=seq),
        grid=(batch, ATT_KV_HEADS, n_q + (1 if ctx_out else 0)),
        scratch_shapes=[pltpu.VMEM((group_w // ATT_HD * tq, seq + ctx_len), F32),
                        pltpu.VMEM((group_w // ATT_HD * tq, seq + ctx_len), BF16),
                        pltpu.VMEM((seq + ctx_len, 2 * ATT_HD), BF16)],
        in_specs=[
            pl.BlockSpec((tq, group_w), q_map),
            pl.BlockSpec((seq, ATT_HD), lambda b, kh, qi: (b, k_col + kh)),
            pl.BlockSpec((seq, ATT_HD), lambda b, kh, qi: (b, v_col + kh)),
            pl.BlockSpec((ctx_len, ATT_HD), lambda b, kh, qi: (ctx_row0 + b, k_col + kh)),
            pl.BlockSpec((ctx_len, ATT_HD), lambda b, kh, qi: (ctx_row0 + b, v_col + kh)),
        ],
        out_specs=pl.BlockSpec((tq, group_w), q_map),
        out_shape=jax.ShapeDtypeStruct((n_rows, ATT_HEADS * ATT_HD), BF16),
        compiler_params=_params(3),
        name="att",
    )(proj, proj, proj, proj, proj)


def _oproj_kernel(x_ref, mod_ref, w_ref, *rest, n_lat):
    gate = mod_ref[0, 0, 5:6, :]
    if len(rest) == 2:
        a_ref, o_ref = rest
        o_ref[...] = x_ref[...] + gate * _dot(a_ref[...], w_ref[...])
        return
    al_ref, ac_ref, o_ref = rest
    i = pl.program_id(0)

    @pl.when(i < n_lat)
    def _():
        o_ref[...] = x_ref[...] + gate * _dot(al_ref[...], w_ref[...])

    @pl.when(i >= n_lat)
    def _():
        o_ref[...] = x_ref[...] + gate * _dot(ac_ref[...], w_ref[...])


def _oproj_call(rows, latent_only, x, mod, layer, w, which, a_lat, a_ctx=None):
    d = x.shape[1]
    k = w.shape[-2]
    n_lat = rows.n_lat
    n_tiles = n_lat if latent_only else rows.n_all
    in_specs = [rows.row_spec(d), rows.mod_spec(layer, d), _resident(w, (which,))]
    args = [x, mod, w]
    if a_ctx is None:
        in_specs.append(rows.row_spec(k))
        args.append(a_lat)
    else:
        in_specs.append(pl.BlockSpec((rows.tile, k), lambda i: (jnp.minimum(i, n_lat - 1), 0)))
        in_specs.append(pl.BlockSpec((rows.tile, k), lambda i: (jnp.maximum(i - n_lat, 0), 0)))
        args += [a_lat, a_ctx]
    return pl.pallas_call(
        functools.partial(_oproj_kernel, n_lat=n_lat),
        grid=(n_tiles,),
        in_specs=in_specs,
        out_specs=rows.row_spec(d),
        out_shape=jax.ShapeDtypeStruct((n_tiles * rows.tile, d), F32),
        compiler_params=_params(1),
        name="oproj",
    )(*args)


def _with_identity_rows(cos, sin, pad_rows):
    pad = (pad_rows, cos.shape[1])
    return (jnp.concatenate([cos, jnp.ones(pad, F32)], axis=0),
            jnp.concatenate([sin, jnp.zeros(pad, F32)], axis=0))


def _seq_rope_tables(seq, pad_rows):
    half = RET_DK // 2
    freqs = ROPE_THETA ** (-jnp.arange(half, dtype=F32) / half)
    ang = jnp.arange(seq, dtype=F32)[:, None] * freqs
    return _with_identity_rows(jnp.cos(ang), jnp.sin(ang), pad_rows)


def _axial_rope_tables(seq, pad_rows):
    quarter = ATT_HD // 4
    tok = jnp.arange(seq)
    r = (tok // GRID_W).astype(F32)
    cl = (tok % GRID_W).astype(F32)
    freqs = ROPE_THETA ** (-jnp.arange(quarter, dtype=F32) / quarter)
    ang = jnp.concatenate([r[:, None] * freqs, cl[:, None] * freqs], axis=-1)
    cos, sin = jnp.cos(ang), jnp.sin(ang)
    return _with_identity_rows(jnp.concatenate([cos, cos], axis=-1),
                               jnp.concatenate([-sin, sin], axis=-1), pad_rows)


def _swap_head_halves(a):
    lead = a.shape[:-1]
    return a.reshape(lead + (-1, 2, ATT_HD // 2))[..., ::-1, :].reshape(a.shape)


def _with_swapped_halves(gain):
    return jnp.stack([gain, _swap_head_halves(gain)], axis=0)


def kernel(x, c, ctx, c_ctx, ada_w, ada_b, norm_g, ffn_w1, ffn_w2, ret_w_in, ret_w_out,
           ret_decay_f, ret_decay_b, att_w_qkv, att_w_o, att_q_gain, att_k_gain, final_g):
    batch, seq, d = x.shape
    ctx_len = ctx.shape[1]
    depth = ada_w.shape[0]
    rows_wide =_Rows.widest(batch, seq, ctx_len, WIDE_TILE_FACTOR)

    cc = jnp.concatenate(
        [c, c_ctx[None], jnp.zeros((ADA_ROWS - batch - 1, d), F32)], axis=0)
    mod = _ada_call(cc, ada_w, ada_b).reshape(depth, ADA_ROWS, N_MOD, d)

    w1 = ffn_w1.astype(BF16)
    w2 = ffn_w2.astype(BF16)
    w_in = ret_w_in.astype(BF16)
    w_out = ret_w_out.astype(BF16)
    w_qkv = att_w_qkv.astype(BF16)
    w_o = att_w_o.astype(BF16)

    rows_att = _Rows.widest(batch, seq, ctx_len, PROJ_ATT_SUBTILES)
    ret_cos, ret_sin = _seq_rope_tables(seq, rows_wide.tile)
    att_cos, att_sin = _axial_rope_tables(seq, rows_att.tile)
    ret_rope_specs = [rows_wide.rope_spec(RET_DK // 2)] * 2
    att_rope_specs = [rows_att.rope_spec(ATT_HD)] * 2
    gain_spec = pl.BlockSpec((2, ATT_HD), lambda i: (0, 0))
    w_qk_swapped = _swap_head_halves(w_qkv[..., :(ATT_HEADS + ATT_KV_HEADS) * ATT_HD])

    xs = x.reshape(batch * seq, d)
    xs_ctx = ctx.reshape(batch * ctx_len, d)

    for i in range(depth):
        last = i == depth - 1
        j = i // 2
        xs = _ffn_call(rows_wide, False, xs, mod, i, 0, norm_g[i, 0], w1, w2, 0,
                       x_ctx=xs_ctx if i == 0 else None)
        if i % 2 == 0:
            proj = _proj_call(_proj_ret_kernel, "proj_ret", rows_wide, xs, mod, i, norm_g[i, 1],
                              w_in, j, [ret_cos, ret_sin], ret_rope_specs)
            dec_tab = jnp.broadcast_to(
                jnp.stack([ret_decay_f[j], ret_decay_b[j]], axis=1)[:, :, None],
                (RET_HEADS, 2, 128))
            dec_tab = jnp.concatenate([dec_tab, jnp.zeros((RET_HEADS, 6, 128), F32)], axis=1)
            a_lat, a_ctx = _ret_call(batch, seq, ctx_len, proj, dec_tab)
            xs = _oproj_call(rows_wide, last, xs, mod, i, w_out, j, a_lat,
                             None if last else a_ctx)
            mixer = None
        else:
            proj = _proj_call(_proj_att_kernel, "proj_att", rows_att, xs, mod, i, norm_g[i, 1],
                              w_qkv, j,
                              [w_qk_swapped, _with_swapped_halves(att_q_gain[j]),
                               _with_swapped_halves(att_k_gain[j]), att_cos, att_sin],
                              [_resident(w_qk_swapped, (j,)), gain_spec, gain_spec]
                              + att_rope_specs)
            mixer = (_att_call(batch, seq, ctx_len, proj, not last), w_o, j)
        xs = _ffn_call(rows_wide, last, xs, mod, i, 6, norm_g[i, 2], w1, w2, 1,
                       final_g if last else None, mixer=mixer)
    return xs.reshape(batch, seq, d)
```

```python
import functools

import jax
import jax.numpy as jnp
from jax import lax
from jax.experimental import pallas as pl
from jax.experimental.pallas import tpu as pltpu

F32 = jnp.float32
BF16 = jnp.bfloat16

N_MOD = 9
RET_HEADS = 4
RET_DK = 256
RET_DV = 512
ATT_HEADS = 8
ATT_KV_HEADS = 2
ATT_HD = 128
GRID_W = 64
ROPE_THETA = 10000.0
EPS = 1e-6

ROW_TILE = 512
RET_CHUNK = 256
ATT_Q_TILE = 256
ATT_KEY_CHUNK = 512
ATT_EXP2_SCALE = ATT_HD ** -0.5 * 1.4426950408889634
LANES = 128
MXU_DEPTH = 256
FFN_SPLIT = 3
WIDE_TILE_FACTOR = 2
PROJ_ATT_SUBTILES = 2
ADA_ROWS = 16
VMEM_LIMIT = 58 * 1024 * 1024


def _params(n_axes):
    return pltpu.CompilerParams(
        dimension_semantics=("arbitrary",) * n_axes, vmem_limit_bytes=VMEM_LIMIT)


def _resident(stacked, lead):
    tail = stacked.shape[len(lead):]
    index = tuple(lead) + (0,) * len(tail)
    return pl.BlockSpec((None,) * len(lead) + tail, lambda *_: index,
                        pipeline_mode=pl.Buffered(1))


def _silu(x):
    return x * jax.nn.sigmoid(x)


def _rms_rows(x):
    return x * lax.rsqrt(jnp.mean(x * x, axis=-1, keepdims=True) + EPS)


def _modulated(x, gain, shift, scale):
    return (_rms_rows(x) * gain) * (1.0 + scale) + shift


def _dot(a, b):
    return jnp.dot(a, b, preferred_element_type=F32)


def _dot_nt(a, b):
    return lax.dot_general(a, b, (((1,), (1,)), ((), ())), preferred_element_type=F32)


def _dot_tn(a, b):
    return lax.dot_general(a, b, (((0,), (0,)), ((), ())), preferred_element_type=F32)


def _ada_kernel(cc_ref, w_ref, b_ref, o_ref):
    s = _silu(cc_ref[...])
    o_ref[0] = _dot(s.astype(BF16), w_ref[0].astype(BF16)) + b_ref[0]


def _ada_call(cc, ada_w, ada_b):
    depth, d, n = ada_w.shape
    tn = 1024
    return pl.pallas_call(
        _ada_kernel,
        grid=(depth, n // tn),
        in_specs=[
            pl.BlockSpec((ADA_ROWS, d), lambda l, j: (0, 0)),
            pl.BlockSpec((1, d, tn), lambda l, j: (l, 0, j)),
            pl.BlockSpec((1, 1, tn), lambda l, j: (l, 0, j)),
        ],
        out_specs=pl.BlockSpec((1, ADA_ROWS, tn), lambda l, j: (l, 0, j)),
        out_shape=jax.ShapeDtypeStruct((depth, ADA_ROWS, n), F32),
        compiler_params=_params(2),
        name="ada",
    )(cc, ada_w, ada_b.reshape(depth, 1, n))


class _Rows:
    def __init__(self, batch, seq, ctx_len, tile=ROW_TILE):
        self.batch, self.seq, self.ctx_len, self.tile = batch, seq, ctx_len, tile
        assert seq % tile == 0 and (batch * ctx_len) % tile == 0
        self.tiles_per_batch = seq // tile
        self.n_lat = batch * seq // tile
        self.n_ctx = batch * ctx_len // tile
        self.n_all = self.n_lat + self.n_ctx

    @classmethod
    def widest(cls, batch, seq, ctx_len, factor):
        tile = ROW_TILE * factor
        if seq % tile or (batch * ctx_len) % tile:
            tile = ROW_TILE
        return cls(batch, seq, ctx_len, tile)

    def mod_spec(self, layer, d):
        n_lat, tpb, batch = self.n_lat, self.tiles_per_batch, self.batch
        return pl.BlockSpec(
            (1, 1, N_MOD, d),
            lambda i: (layer, jnp.where(i < n_lat, i // tpb, batch), 0, 0))

    def rope_spec(self, width):
        n_lat, tpb = self.n_lat, self.tiles_per_batch
        return pl.BlockSpec((self.tile, width), lambda i: (jnp.where(i < n_lat, i % tpb, tpb), 0))

    def row_spec(self, width):
        return pl.BlockSpec((self.tile, width), lambda i: (i, 0))


def _mod_rows(mod_ref, base):
    return [mod_ref[0, 0, base + k:base + k + 1, :] for k in range(3)]


def _ffn_kernel(*refs, base, ffn_dim, chunks, final, n_lat_split, mixer):
    refs = list(refs)
    o_ref = refs.pop()
    fg_ref = refs.pop() if final else None
    x_ref = refs.pop(0)
    xc_ref = refs.pop(0) if n_lat_split is not None else None
    a_ref, wm_ref = (refs.pop(0), refs.pop(0)) if mixer else (None, None)
    mod_ref, g_ref, w1_ref, w2_ref = refs
    shift, scale, gate = _mod_rows(mod_ref, base)
    x = x_ref[...]
    if xc_ref is not None:
        x = jnp.where(pl.program_id(0) < n_lat_split, x, xc_ref[...])
    if mixer:
        x = x + mod_ref[0, 0, 5:6, :] * _dot(a_ref[...], wm_ref[...])
    h = _modulated(x, g_ref[...], shift, scale).astype(BF16)
    acc = None
    lo = 0
    for width in chunks:
        gt = _dot(h, w1_ref[:, lo:lo + width])
        up = _dot(h, w1_ref[:, ffn_dim + lo:ffn_dim + lo + width])
        a = (_silu(gt) * up).astype(BF16)
        part = _dot(a, w2_ref[lo:lo + width, :])
        acc = part if acc is None else acc + part
        lo += width
    y = x + (0.5 * gate) * acc
    if final:
        y = _rms_rows(y) * fg_ref[...]
    o_ref[...] = y


def _ffn_chunks(ffn_dim):
    assert ffn_dim % MXU_DEPTH == 0
    n_tiles = ffn_dim // MXU_DEPTH
    first = (n_tiles // FFN_SPLIT) * MXU_DEPTH
    sizes = [first] * (FFN_SPLIT - 1) + [ffn_dim - first * (FFN_SPLIT - 1)]
    return tuple(s for s in sizes if s)


def _ffn_call(rows, latent_only, x, mod, layer, base, gain, w1, w2, which, final_g=None,
              x_ctx=None, mixer=None):
    d = x.shape[1]
    ffn_dim = w2.shape[-2]
    chunks = _ffn_chunks(ffn_dim)
    final = final_g is not None
    n_lat = rows.n_lat
    n_tiles = n_lat if latent_only else rows.n_all
    if x_ctx is None:
        in_specs, args = [rows.row_spec(d)], [x]
    else:
        in_specs = [pl.BlockSpec((rows.tile, d), lambda i: (jnp.minimum(i, n_lat - 1), 0)),
                    pl.BlockSpec((rows.tile, d), lambda i: (jnp.maximum(i - n_lat, 0), 0))]
        args = [x, x_ctx]
    if mixer is not None:
        a, w_mix, mix_idx = mixer
        in_specs += [rows.row_spec(a.shape[1]), _resident(w_mix, (mix_idx,))]
        args += [a, w_mix]
    in_specs += [rows.mod_spec(layer, d), pl.BlockSpec((1, d), lambda i: (0, 0)),
                 _resident(w1, (layer, which)), _resident(w2, (layer, which))]
    args += [mod, gain.reshape(1, d), w1, w2]
    if final:
        in_specs.append(pl.BlockSpec((1, d), lambda i: (0, 0)))
        args.append(final_g.reshape(1, d))
    return pl.pallas_call(
        functools.partial(_ffn_kernel, base=base, ffn_dim=ffn_dim, chunks=chunks, final=final,
                          n_lat_split=None if x_ctx is None else n_lat,
                          mixer=mixer is not None),
        grid=(n_tiles,),
        in_specs=in_specs,
        out_specs=rows.row_spec(d),
        out_shape=jax.ShapeDtypeStruct((n_tiles * rows.tile, d), F32),
        compiler_params=_params(1),
        name="ffn",
    )(*args)


def _rope_halves(x1, x2, cos, sin):
    return x1 * cos - x2 * sin, x2 * cos + x1 * sin


def _proj_ret_kernel(x_ref, mod_ref, g_ref, w_ref, cos_ref, sin_ref, o_ref):
    shift, scale, _ = _mod_rows(mod_ref, 3)
    qd = RET_HEADS * RET_DK
    vd = RET_HEADS * RET_DV
    half = RET_DK // 2
    v_lo, g_lo = 2 * qd, 2 * qd + vd
    for r0 in range(0, o_ref.shape[0], ROW_TILE):
        rs = slice(r0, r0 + ROW_TILE)
        h = _modulated(x_ref[rs, :], g_ref[...], shift, scale).astype(BF16)
        cos, sin = cos_ref[rs, :], sin_ref[rs, :]
        o_ref[rs, g_lo:g_lo + vd] = _silu(_dot(h, w_ref[:, g_lo:g_lo + vd])).astype(BF16)
        for part, mult in ((0, None), (1, RET_DK ** -0.5)):
            p = _dot(h, w_ref[:, part * qd:(part + 1) * qd])
            if mult is not None:
                p = p * mult
            for hh in range(RET_HEADS):
                lo = hh * RET_DK
                r1, r2 = _rope_halves(p[:, lo:lo + half], p[:, lo + half:lo + RET_DK], cos, sin)
                o_ref[rs, part * qd + lo:part * qd + lo + half] = r1.astype(BF16)
                o_ref[rs, part * qd + lo + half:part * qd + lo + RET_DK] = r2.astype(BF16)
        o_ref[rs, v_lo:g_lo] = _dot(h, w_ref[:, v_lo:g_lo]).astype(BF16)


def _proj_att_kernel(x_ref, mod_ref, g_ref, w_ref, wr_ref, qg_ref, kg_ref, cos_ref, sin_ref, o_ref):
    shift, scale, _ = _mod_rows(mod_ref, 3)
    qd = ATT_HEADS * ATT_HD
    kd = ATT_KV_HEADS * ATT_HD
    for r0 in range(0, o_ref.shape[0], ROW_TILE):
        rs = slice(r0, r0 + ROW_TILE)
        h = _modulated(x_ref[rs, :], g_ref[...], shift, scale).astype(BF16)
        cos, sin = cos_ref[rs, :], sin_ref[rs, :]

        def norm_rope(lo_col, n_heads, gains, gain_scale):
            p = _dot(h, w_ref[:, lo_col:lo_col + n_heads * ATT_HD])
            p_swapped = _dot(h, wr_ref[:, lo_col:lo_col + n_heads * ATT_HD])
            gain_cos = (gains[0:1, :] * gain_scale) * cos
            gain_sin = (gains[1:2, :] * gain_scale) * sin
            for hh in range(n_heads):
                lo = hh * ATT_HD
                ph = p[:, lo:lo + ATT_HD]
                inv = lax.rsqrt(jnp.mean(ph * ph, axis=-1, keepdims=True) + EPS)
                r = (ph * gain_cos + p_swapped[:, lo:lo + ATT_HD] * gain_sin) * inv
                o_ref[rs, lo_col + lo:lo_col + lo + ATT_HD] = r.astype(BF16)

        norm_rope(0, ATT_HEADS, qg_ref[...], ATT_EXP2_SCALE)
        norm_rope(qd, ATT_KV_HEADS, kg_ref[...], 1.0)
        o_ref[rs, qd + kd:] = _dot(h, w_ref[:, qd + kd:]).astype(BF16)


def _proj_call(kernel, name, rows, x, mod, layer, gain, w, which, extras, extra_specs):
    d = x.shape[1]
    n_out = w.shape[-1]
    return pl.pallas_call(
        kernel,
        grid=(rows.n_all,),
        in_specs=[rows.row_spec(d), rows.mod_spec(layer, d), pl.BlockSpec((1, d), lambda i: (0, 0)),
                  _resident(w, (which,))] + extra_specs,
        out_specs=rows.row_spec(n_out),
        out_shape=jax.ShapeDtypeStruct((rows.n_all * rows.tile, n_out), BF16),
        compiler_params=_params(1),
        name=name,
    )(x, mod, gain.reshape(1, d), w, *extras)


def _decay_tables(n, lg_f, lg_b):
    i = lax.broadcasted_iota(jnp.int32, (n, n), 0)
    j = lax.broadcasted_iota(jnp.int32, (n, n), 1)
    diff = (i - j).astype(F32)
    mask = (jnp.where(diff >= 0, jnp.exp(jnp.maximum(diff, 0.0) * lg_f), 0.0)
            + jnp.where(diff <= 0, jnp.exp(jnp.maximum(-diff, 0.0) * lg_b), 0.0))
    pos = lax.broadcasted_iota(jnp.int32, (n, 1), 0).astype(F32)
    xi_f = jnp.exp((pos + 1.0) * lg_f)
    xi_b = jnp.exp((n - pos) * lg_b)
    zeta_f = jnp.exp((n - 1.0 - pos) * lg_f)
    zeta_b = jnp.exp(pos * lg_b)
    return mask, xi_f, xi_b, zeta_f, zeta_b


def _scaled_bf16(x, col):
    return (x.astype(F32) * col).astype(BF16)


def _head_norm_gate(o, silu_g):
    return (_rms_rows(o) * silu_g.astype(F32)).astype(BF16)


def _scaled_pair(x, col_a, col_b):
    xf = x.astype(F32)
    return jnp.concatenate([(xf * col_a).astype(BF16), (xf * col_b).astype(BF16)], axis=1)


def _ret_kernel(ql_ref, kl_ref, vl_ref, gl_ref, qc_ref, kc_ref, vc_ref, gc_ref, dec_ref,
                ol_ref, oc_ref, st_ref, ub_ref, runf_ref, runb_ref, mask_ref, cmask_ref,
                col_ref, ccol_ref, *, n_chunks, chunk, ctx_len):
    dk = RET_DK
    lg_f = -jnp.exp(dec_ref[0, 0:1, 0:1])
    lg_b = -jnp.exp(dec_ref[0, 1:2, 0:1])

    @pl.when(pl.program_id(1) == 0)
    def _():
        mask, xi_f, xi_b, zeta_f, zeta_b = _decay_tables(chunk, lg_f, lg_b)
        mask_ref[...] = mask
        for k, col in enumerate((xi_f, xi_b, zeta_f, zeta_b)):
            col_ref[k] = col
        cmask, _, _, czeta_f, czeta_b = _decay_tables(ctx_len, lg_f, lg_b)
        cmask_ref[...] = cmask
        ccol_ref[0] = czeta_f
        ccol_ref[1] = czeta_b

    decay_f = jnp.exp(chunk * lg_f)
    decay_b = jnp.exp(chunk * lg_b)

    q, k, v = qc_ref[...], kc_ref[...], vc_ref[...]
    p = (_dot_nt(q, k) * cmask_ref[...]).astype(BF16)
    oc_ref[...] = _head_norm_gate(_dot(p, v), gc_ref[...])
    s0 = _dot_tn(_scaled_pair(k, ccol_ref[0], ccol_ref[1]), v)
    runf_ref[...] = s0[:dk]
    runb_ref[...] = s0[dk:]
    st_ref[0, :dk, :] = s0[:dk].astype(BF16)
    st_ref[n_chunks - 1, dk:, :] = s0[dk:].astype(BF16)

    def rows(n):
        return slice(n * chunk, (n + 1) * chunk)

    for n in range(n_chunks):
        u = _dot_tn(_scaled_pair(kl_ref[rows(n), :], col_ref[2], col_ref[3]), vl_ref[rows(n), :])
        if n + 1 < n_chunks:
            s = runf_ref[...] * decay_f + u[:dk]
            runf_ref[...] = s
            st_ref[n + 1, :dk, :] = s.astype(BF16)
        if n > 0:
            ub_ref[n] = u[dk:]

    for n in range(n_chunks - 1, 0, -1):
        s = runb_ref[...] * decay_b + ub_ref[n]
        runb_ref[...] = s
        st_ref[n - 1, dk:, :] = s.astype(BF16)

    for n in range(n_chunks):
        r = rows(n)
        q, k, v = ql_ref[r, :], kl_ref[r, :], vl_ref[r, :]
        p = (_dot_nt(q, k) * mask_ref[...]).astype(BF16)
        o = _dot(p, v) + _dot(_scaled_pair(q, col_ref[0], col_ref[1]), st_ref[n])
        ol_ref[r, :] = _head_norm_gate(o, gl_ref[r, :])


def _ret_call(batch, seq, ctx_len, proj, dec_tab):
    chunk = min(RET_CHUNK, seq)
    assert seq % chunk == 0
    n_chunks = seq // chunk
    ctx_row0 = batch * seq // ctx_len
    kq = RET_HEADS
    kv = 2 * RET_HEADS * RET_DK // RET_DV
    kg = kv + RET_HEADS
    vd = RET_HEADS * RET_DV
    in_specs = [
        pl.BlockSpec((seq, RET_DK), lambda h, b: (b, h)),
        pl.BlockSpec((seq, RET_DK), lambda h, b: (b, kq + h)),
        pl.BlockSpec((seq, RET_DV), lambda h, b: (b, kv + h)),
        pl.BlockSpec((seq, RET_DV), lambda h, b: (b, kg + h)),
        pl.BlockSpec((ctx_len, RET_DK), lambda h, b: (ctx_row0 + b, h)),
        pl.BlockSpec((ctx_len, RET_DK), lambda h, b: (ctx_row0 + b, kq + h)),
        pl.BlockSpec((ctx_len, RET_DV), lambda h, b: (ctx_row0 + b, kv + h)),
        pl.BlockSpec((ctx_len, RET_DV), lambda h, b: (ctx_row0 + b, kg + h)),
        pl.BlockSpec((1, 8, 128), lambda h, b: (h, 0, 0)),
    ]
    return pl.pallas_call(
        functools.partial(_ret_kernel, n_chunks=n_chunks, chunk=chunk, ctx_len=ctx_len),
        grid=(RET_HEADS, batch),
        in_specs=in_specs,
        out_specs=[pl.BlockSpec((seq, RET_DV), lambda h, b: (b, h)),
                   pl.BlockSpec((ctx_len, RET_DV), lambda h, b: (b, h))],
        out_shape=[jax.ShapeDtypeStruct((batch * seq, vd), BF16),
                   jax.ShapeDtypeStruct((batch * ctx_len, vd), BF16)],
        scratch_shapes=[pltpu.VMEM((n_chunks, 2 * RET_DK, RET_DV), BF16),
                        pltpu.VMEM((n_chunks, RET_DK, RET_DV), F32),
                        pltpu.VMEM((RET_DK, RET_DV), F32),
                        pltpu.VMEM((RET_DK, RET_DV), F32),
                        pltpu.VMEM((chunk, chunk), F32),
                        pltpu.VMEM((ctx_len, ctx_len), F32),
                        pltpu.VMEM((4, chunk, 1), F32),
                        pltpu.VMEM((2, ctx_len, 1), F32)],
        compiler_params=_params(2),
        name="ret",
    )(proj, proj, proj, proj, proj, proj, proj, proj, dec_tab)


def _att_kernel(q_ref, kl_ref, vl_ref, kc_ref, vc_ref, o_ref, s_ref, p_ref, va_ref, *, n_q, seq):
    qi = pl.program_id(2)
    group = ATT_HEADS // ATT_KV_HEADS
    tq = q_ref.shape[0]
    n_keys = s_ref.shape[1]

    @pl.when(qi == 0)
    def _():
        va_ref[:seq, :ATT_HD] = vl_ref[...]
        va_ref[seq:, :ATT_HD] = vc_ref[...]
        va_ref[:, ATT_HD:] = jnp.ones((n_keys, ATT_HD), BF16)

    def attend(col_lo):
        slabs = range(col_lo, n_keys, LANES)
        for g in range(group):
            q = q_ref[:, g * ATT_HD:(g + 1) * ATT_HD]
            if col_lo < seq:
                s_ref[g * tq:(g + 1) * tq, :seq] = _dot_nt(q, kl_ref[...])
            s_ref[g * tq:(g + 1) * tq, seq:] = _dot_nt(q, kc_ref[...])
        for g in range(group):
            rs = slice(g * tq, (g + 1) * tq)
            m = None
            for c0 in slabs:
                t = s_ref[rs, c0:c0 + LANES]
                m = t if m is None else jnp.maximum(m, t)
            m = jnp.max(m, axis=-1, keepdims=True)
            o = None
            for k0 in range(col_lo, n_keys, ATT_KEY_CHUNK):
                k1 = min(k0 + ATT_KEY_CHUNK, seq if k0 < seq else n_keys)
                for c0 in range(k0, k1, LANES):
                    e = jnp.exp2(s_ref[rs, c0:c0 + LANES] - m)
                    p_ref[rs, c0:c0 + LANES] = e.astype(BF16)
                part = _dot(p_ref[rs, k0:k1], va_ref[k0:k1, :])
                o = part if o is None else o + part
            o_ref[:, g * ATT_HD:(g + 1) * ATT_HD] = (o[:, :ATT_HD] / o[:, ATT_HD:]).astype(BF16)

    @pl.when(qi < n_q)
    def _():
        attend(0)

    @pl.when(qi >= n_q)
    def _():
        attend(seq)


def _att_call(batch, seq, ctx_len, proj, ctx_out):
    tq = ATT_Q_TILE
    assert ctx_len == tq and seq % tq == 0
    n_q = seq // tq
    ctx_row0 = batch * seq // ctx_len
    group_w = (ATT_HEADS // ATT_KV_HEADS) * ATT_HD
    k_col = ATT_HEADS
    v_col = ATT_HEADS + ATT_KV_HEADS
    n_rows = batch * seq + (batch * ctx_len if ctx_out else 0)

    def q_map(b, kh, qi):
        return (jnp.where(qi < n_q, b * n_q + qi, ctx_row0 + b), kh)

    return pl.pallas_call(
        functools.partial(_att_kernel, n_q=n_q, seq=seq),
        grid=(batch, ATT_KV_HEADS, n_q + (1 if ctx_out else 0)),
        scratch_shapes=[pltpu.VMEM((group_w // ATT_HD * tq, seq + ctx_len), F32),
                        pltpu.VMEM((group_w // ATT_HD * tq, seq + ctx_len), BF16),
                        pltpu.VMEM((seq + ctx_len, 2 * ATT_HD), BF16)],
        in_specs=[
            pl.BlockSpec((tq, group_w), q_map),
            pl.BlockSpec((seq, ATT_HD), lambda b, kh, qi: (b, k_col + kh)),
            pl.BlockSpec((seq, ATT_HD), lambda b, kh, qi: (b, v_col + kh)),
            pl.BlockSpec((ctx_len, ATT_HD), lambda b, kh, qi: (ctx_row0 + b, k_col + kh)),
            pl.BlockSpec((ctx_len, ATT_HD), lambda b, kh, qi: (ctx_row0 + b, v_col + kh)),
        ],
        out_specs=pl.BlockSpec((tq, group_w), q_map),
        out_shape=jax.ShapeDtypeStruct((n_rows, ATT_HEADS * ATT_HD), BF16),
        compiler_params=_params(3),
        name="att",
    )(proj, proj, proj, proj, proj)


def _oproj_kernel(x_ref, mod_ref, w_ref, *rest, n_lat):
    gate = mod_ref[0, 0, 5:6, :]
    if len(rest) == 2:
        a_ref, o_ref = rest
        o_ref[...] = x_ref[...] + gate * _dot(a_ref[...], w_ref[...])
        return
    al_ref, ac_ref, o_ref = rest
    i = pl.program_id(0)

    @pl.when(i < n_lat)
    def _():
        o_ref[...] = x_ref[...] + gate * _dot(al_ref[...], w_ref[...])

    @pl.when(i >= n_lat)
    def _():
        o_ref[...] = x_ref[...] + gate * _dot(ac_ref[...], w_ref[...])


def _oproj_call(rows, latent_only, x, mod, layer, w, which, a_lat, a_ctx=None):
    d = x.shape[1]
    k = w.shape[-2]
    n_lat = rows.n_lat
    n_tiles = n_lat if latent_only else rows.n_all
    in_specs = [rows.row_spec(d), rows.mod_spec(layer, d), _resident(w, (which,))]
    args = [x, mod, w]
    if a_ctx is None:
        in_specs.append(rows.row_spec(k))
        args.append(a_lat)
    else:
        in_specs.append(pl.BlockSpec((rows.tile, k), lambda i: (jnp.minimum(i, n_lat - 1), 0)))
        in_specs.append(pl.BlockSpec((rows.tile, k), lambda i: (jnp.maximum(i - n_lat, 0), 0)))
        args += [a_lat, a_ctx]
    return pl.pallas_call(
        functools.partial(_oproj_kernel, n_lat=n_lat),
        grid=(n_tiles,),
        in_specs=in_specs,
        out_specs=rows.row_spec(d),
        out_shape=jax.ShapeDtypeStruct((n_tiles * rows.tile, d), F32),
        compiler_params=_params(1),
        name="oproj",
    )(*args)


def _with_identity_rows(cos, sin, pad_rows):
    pad = (pad_rows, cos.shape[1])
    return (jnp.concatenate([cos, jnp.ones(pad, F32)], axis=0),
            jnp.concatenate([sin, jnp.zeros(pad, F32)], axis=0))


def _seq_rope_tables(seq, pad_rows):
    half = RET_DK // 2
    freqs = ROPE_THETA ** (-jnp.arange(half, dtype=F32) / half)
    ang = jnp.arange(seq, dtype=F32)[:, None] * freqs
    return _with_identity_rows(jnp.cos(ang), jnp.sin(ang), pad_rows)


def _axial_rope_tables(seq, pad_rows):
    quarter = ATT_HD // 4
    tok = jnp.arange(seq)
    r = (tok // GRID_W).astype(F32)
    cl = (tok % GRID_W).astype(F32)
    freqs = ROPE_THETA ** (-jnp.arange(quarter, dtype=F32) / quarter)
    ang = jnp.concatenate([r[:, None] * freqs, cl[:, None] * freqs], axis=-1)
    cos, sin = jnp.cos(ang), jnp.sin(ang)
    return _with_identity_rows(jnp.concatenate([cos, cos], axis=-1),
                               jnp.concatenate([-sin, sin], axis=-1), pad_rows)


def _swap_head_halves(a):
    lead = a.shape[:-1]
    return a.reshape(lead + (-1, 2, ATT_HD // 2))[..., ::-1, :].reshape(a.shape)


def _with_swapped_halves(gain):
    return jnp.stack([gain, _swap_head_halves(gain)], axis=0)


def kernel(x, c, ctx, c_ctx, ada_w, ada_b, norm_g, ffn_w1, ffn_w2, ret_w_in, ret_w_out,
           ret_decay_f, ret_decay_b, att_w_qkv, att_w_o, att_q_gain, att_k_gain, final_g):
    batch, seq, d = x.shape
    ctx_len = ctx.shape[1]
    depth = ada_w.shape[0]
    rows_wide = _Rows.widest(batch, seq, ctx_len, WIDE_TILE_FACTOR)

    cc = jnp.concatenate(
        [c, c_ctx[None], jnp.zeros((ADA_ROWS - batch - 1, d), F32)], axis=0)
    mod = _ada_call(cc, ada_w, ada_b).reshape(depth, ADA_ROWS, N_MOD, d)

    w1 = ffn_w1.astype(BF16)
    w2 = ffn_w2.astype(BF16)
    w_in = ret_w_in.astype(BF16)
    w_out = ret_w_out.astype(BF16)
    w_qkv = att_w_qkv.astype(BF16)
    w_o = att_w_o.astype(BF16)

    rows_att = _Rows.widest(batch, seq, ctx_len, PROJ_ATT_SUBTILES)
    ret_cos, ret_sin = _seq_rope_tables(seq, rows_wide.tile)
    att_cos, att_sin = _axial_rope_tables(seq, rows_att.tile)
    ret_rope_specs = [rows_wide.rope_spec(RET_DK // 2)] * 2
    att_rope_specs = [rows_att.rope_spec(ATT_HD)] * 2
    gain_spec = pl.BlockSpec((2, ATT_HD), lambda i: (0, 0))
    w_qk_swapped = _swap_head_halves(w_qkv[..., :(ATT_HEADS + ATT_KV_HEADS) * ATT_HD])

    xs = x.reshape(batch * seq, d)
    xs_ctx = ctx.reshape(batch * ctx_len, d)

    for i in range(depth):
        last = i == depth - 1
        j = i // 2
        xs = _ffn_call(rows_wide, False, xs, mod, i, 0, norm_g[i, 0], w1, w2, 0,
                       x_ctx=xs_ctx if i == 0 else None)
        if i % 2 == 0:
            proj = _proj_call(_proj_ret_kernel, "proj_ret", rows_wide, xs, mod, i, norm_g[i, 1],
                              w_in, j, [ret_cos, ret_sin], ret_rope_specs)
            dec_tab = jnp.broadcast_to(
                jnp.stack([ret_decay_f[j], ret_decay_b[j]], axis=1)[:, :, None],
                (RET_HEADS, 2, 128))
            dec_tab = jnp.concatenate([dec_tab, jnp.zeros((RET_HEADS, 6, 128), F32)], axis=1)
            a_lat, a_ctx = _ret_call(batch, seq, ctx_len, proj, dec_tab)
            xs = _oproj_call(rows_wide, last, xs, mod, i, w_out, j, a_lat,
                             None if last else a_ctx)
            mixer = None
        else:
            proj = _proj_call(_proj_att_kernel, "proj_att", rows_att, xs, mod, i, norm_g[i, 1],
                              w_qkv, j,
                              [w_qk_swapped, _with_swapped_halves(att_q_gain[j]),
                               _with_swapped_halves(att_k_gain[j]), att_cos, att_sin],
                              [_resident(w_qk_swapped, (j,)), gain_spec, gain_spec]
                              + att_rope_specs)
            mixer = (_att_call(batch, seq, ctx_len, proj, not last), w_o, j)
        xs = _ffn_call(rows_wide, last, xs, mod, i, 6, norm_g[i, 2], w1, w2, 1,
                       final_g if last else None, mixer=mixer)
    return xs.reshape(batch, seq, d)
```
